```python
import jax, jax.numpy as jnp
from jax import lax
import numpy as np

D_MODEL = 1024
BATCH = 2
SEQ = 8192
DEPTH = 2

CONV_CH = D_MODEL // 2
CONV_WIDTH = 31
N_HEADS = 8
HEAD_DIM = 64
ATTN_WIDTH = N_HEADS * HEAD_DIM
D_FF = ((8 * D_MODEL // 3 + 255) // 256) * 256
Q_BLOCK = 128
EPS = 1e-6

COL_SIZES = (2 * CONV_CH,
             ATTN_WIDTH,
             ATTN_WIDTH,
             ATTN_WIDTH,
             N_HEADS,
             D_MODEL,
             D_MODEL)
IN_COLS = sum(COL_SIZES)
COL_SPLITS = tuple(int(s) for s in np.cumsum(COL_SIZES)[:-1])

kernel_name = "hybrid_conformer_conv_fox_attention_swiglu"


def rmsnorm(x, g):
    xf = x.astype(jnp.float32)
    inv = lax.rsqrt(jnp.mean(xf * xf, axis=-1, keepdims=True) + EPS)
    return (xf * inv).astype(x.dtype) * g


def layernorm(x, g, b):
    xf = x.astype(jnp.float32)
    mu = jnp.mean(xf, axis=-1, keepdims=True)
    var = jnp.mean(jnp.square(xf - mu), axis=-1, keepdims=True)
    return ((xf - mu) * lax.rsqrt(var + EPS)).astype(x.dtype) * g + b


def causal_depthwise_conv(a, w, b):
    out = lax.conv_general_dilated(
        a, w[:, None, :], window_strides=(1,), padding=[(CONV_WIDTH - 1, 0)],
        dimension_numbers=("NWC", "WIO", "NWC"), feature_group_count=CONV_CH)
    return out + b


def conformer_conv_branch(a_in, w_dw, b_dw, g_ln, b_ln, w_out):
    a = a_in[..., :CONV_CH] * jax.nn.sigmoid(a_in[..., CONV_CH:])
    a = causal_depthwise_conv(a, w_dw, b_dw)
    a = jax.nn.silu(layernorm(a, g_ln, b_ln))
    return a @ w_out


def forgetting_attention(q, k, v, f_logit, b_forget, g_q, g_k):
    B, S, _ = q.shape
    nb = S // Q_BLOCK
    def heads(t, g):
        t = t.reshape(B, S, N_HEADS, HEAD_DIM)
        if g is not None:
            t = rmsnorm(t, g)
        return t.transpose(0, 2, 1, 3)
    qh, kh, vh = heads(q, g_q), heads(k, g_k), heads(v, None)
    log_f = jax.nn.log_sigmoid(f_logit.astype(jnp.float32) + b_forget.astype(jnp.float32))
    c = jnp.cumsum(log_f, axis=1).transpose(0, 2, 1)
    scale = HEAD_DIM ** -0.5
    q_blocks = qh.reshape(B, N_HEADS, nb, Q_BLOCK, HEAD_DIM).transpose(2, 0, 1, 3, 4)
    c_blocks = c.reshape(B, N_HEADS, nb, Q_BLOCK).transpose(2, 0, 1, 3)
    k_pos = jnp.arange(S)

    def one_block(args):
        qi, ci, i = args
        s = jnp.einsum("bhqd,bhkd->bhqk", qi, kh).astype(jnp.float32) * scale
        s = s + ci[..., :, None] - c[:, :, None, :]
        q_pos = i * Q_BLOCK + jnp.arange(Q_BLOCK)
        s = jnp.where(k_pos[None, :] <= q_pos[:, None], s, -jnp.inf)
        p = jax.nn.softmax(s, axis=-1).astype(vh.dtype)
        return jnp.einsum("bhqk,bhkd->bhqd", p, vh)

    out = lax.map(one_block, (q_blocks, c_blocks, jnp.arange(nb)))
    return out.transpose(1, 0, 3, 2, 4).reshape(B, S, ATTN_WIDTH)


def setup_inputs(seed: int = 0) -> dict:
    key = jax.random.key(seed)
    ks = jax.random.split(key, 20)
    f32 = jnp.float32
    def nrm(k, shape, fan_in):
        return jax.random.normal(k, shape, f32) * fan_in ** -0.5
    def gain(k, shape):
        return 1.0 + 0.02 * jax.random.normal(k, shape, f32)
    return {
        "x": jax.random.normal(ks[0], (BATCH, SEQ, D_MODEL), f32),
        "g_mix": gain(ks[1], (DEPTH, D_MODEL)),
        "w_in": nrm(ks[2], (DEPTH, D_MODEL, IN_COLS), D_MODEL),
        "b_forget": jax.random.uniform(ks[3], (DEPTH, N_HEADS), f32, 2.0, 5.0),
        "w_dw": nrm(ks[4], (DEPTH, CONV_WIDTH, CONV_CH), CONV_WIDTH),
        "b_dw": 0.02 * jax.random.normal(ks[5], (DEPTH, CONV_CH), f32),
        "g_conv_ln": gain(ks[6], (DEPTH, CONV_CH)),
        "b_conv_ln": 0.02 * jax.random.normal(ks[7], (DEPTH, CONV_CH), f32),
        "w_conv_out": nrm(ks[8], (DEPTH, CONV_CH, D_MODEL), CONV_CH),
        "g_q": gain(ks[9], (DEPTH, HEAD_DIM)),
        "g_k": gain(ks[10], (DEPTH, HEAD_DIM)),
        "w_attn_out": nrm(ks[11], (DEPTH, ATTN_WIDTH, D_MODEL), ATTN_WIDTH),
        "w_out": nrm(ks[12], (DEPTH, D_MODEL, D_MODEL), D_MODEL),
        "g_ffn": gain(ks[13], (DEPTH, D_MODEL)),
        "w_ffn_in": nrm(ks[14], (DEPTH, D_MODEL, 2 * D_FF), D_MODEL),
        "w_ffn_out": nrm(ks[15], (DEPTH, D_FF, D_MODEL), D_FF),
    }


def reference(x, g_mix, w_in, b_forget, w_dw, b_dw, g_conv_ln, b_conv_ln, w_conv_out,
              g_q, g_k, w_attn_out, w_out, g_ffn, w_ffn_in, w_ffn_out):
    for l in range(DEPTH):
        h = rmsnorm(x, g_mix[l])
        proj = h @ w_in[l]
        a_in, q, k, v, f_logit, gate_a, gate_b = jnp.split(proj, COL_SPLITS, axis=-1)
        y_conv = conformer_conv_branch(a_in, w_dw[l], b_dw[l], g_conv_ln[l],
                                       b_conv_ln[l], w_conv_out[l])
        y_attn = forgetting_attention(q, k, v, f_logit, b_forget[l], g_q[l], g_k[l]) @ w_attn_out[l]
        merged = jax.nn.sigmoid(gate_a) * y_conv + jax.nn.sigmoid(gate_b) * y_attn
        x = x + merged @ w_out[l]
        h2 = rmsnorm(x, g_ffn[l])
        gu = h2 @ w_ffn_in[l]
        x = x + (jax.nn.silu(gu[..., :D_FF]) * gu[..., D_FF:]) @ w_ffn_out[l]
    return x
```

```python
import functools

import jax
import jax.numpy as jnp
import numpy as np
from jax import lax
from jax.experimental import pallas as pl
from jax.experimental.pallas import tpu as pltpu

D_MODEL = 1024
CONV_CH = 512
CONV_WIDTH = 31
N_HEADS = 8
HEAD_DIM = 64
ATTN_WIDTH = N_HEADS * HEAD_DIM
D_FF = 2816
EPS = 1e-6

LANES = 128
SUBLANES = 8
HALO = 32
F_PAD = LANES
IN_COLS_PADDED = 2 * CONV_CH + 3 * ATTN_WIDTH + 2 * D_MODEL + F_PAD
VMEM_LIMIT_BYTES = 56 * 1024 * 1024

ROW_TILE = 512
CUMSUM_CHUNK = 256
CONV_ROWS = 32
Q_TILE = 512
KV_TILE = 512
FFN_CHUNKS = (1024, 1024, 768)

BF16 = jnp.bfloat16
F32 = jnp.float32


def _dot(a, b):
    return jnp.dot(a, b, preferred_element_type=F32)


def _split3(x):
    hi = x.astype(BF16)
    r = x - hi.astype(F32)
    mid = r.astype(BF16)
    lo = (r - mid.astype(F32)).astype(BF16)
    return hi, mid, lo


def _sigmoid(x):
    return 1.0 / (1.0 + jnp.exp(-x))


def _const_spec(shape):
    nd = len(shape)
    return pl.BlockSpec(shape, lambda *_: (0,) * nd, pipeline_mode=pl.Buffered(1))


def _in_proj_kernel(x_ref, g_ref, w_ref, bf_ref, gq_ref, gk_ref, tri_ref, place_ref, hsum_ref,
                    a_ref, qa_ref, ka_ref, v_ref, sga_ref, sgb_ref, carry_ref):
    tm = x_ref.shape[1]

    @pl.when(pl.program_id(1) == 0)
    def _():
        carry_ref[...] = jnp.zeros_like(carry_ref)

    x = x_ref[0]
    inv = lax.rsqrt(jnp.mean(x * x, axis=-1, keepdims=True) + EPS)
    h = ((x * inv) * g_ref[...]).astype(BF16)

    c0 = 0
    pa = _dot(h, w_ref[:, c0:c0 + 2 * CONV_CH])
    a_ref[0] = pa[:, :CONV_CH] * _sigmoid(pa[:, CONV_CH:])
    c0 += 2 * CONV_CH

    def head_rmsnorm(p, gain):
        sq = p * p
        hi = sq.astype(BF16)
        lo = (sq - hi.astype(F32)).astype(BF16)
        ms = _dot(hi, hsum_ref[...]) + _dot(lo, hsum_ref[...])
        return (p * lax.rsqrt(ms + EPS)) * gain

    qn = head_rmsnorm(_dot(h, w_ref[:, c0:c0 + ATTN_WIDTH]), gq_ref[...])
    c0 += ATTN_WIDTH
    kn = head_rmsnorm(_dot(h, w_ref[:, c0:c0 + ATTN_WIDTH]), gk_ref[...])
    c0 += ATTN_WIDTH
    v_ref[0] = _dot(h, w_ref[:, c0:c0 + ATTN_WIDTH]).astype(BF16)
    c0 += ATTN_WIDTH
    sga_ref[0] = _sigmoid(_dot(h, w_ref[:, c0:c0 + D_MODEL])).astype(BF16)
    c0 += D_MODEL
    sgb_ref[0] = _sigmoid(_dot(h, w_ref[:, c0:c0 + D_MODEL])).astype(BF16)
    c0 += D_MODEL

    z = _dot(h, w_ref[:, c0:c0 + F_PAD]) + bf_ref[...]
    lane = lax.broadcasted_iota(jnp.int32, (tm, F_PAD), 1)
    logf = jnp.where(lane < N_HEADS, jnp.minimum(z, 0.0) - jnp.log1p(jnp.exp(-jnp.abs(z))), 0.0)
    hi, mid, lo = _split3(logf)
    parts = jnp.concatenate([hi, mid, lo], axis=1)
    carry = carry_ref[0:1, :]
    chunks = []
    for ch in range(tm // CUMSUM_CHUNK):
        y = _dot(tri_ref[...], parts[ch * CUMSUM_CHUNK:(ch + 1) * CUMSUM_CHUNK])
        c = (y[:, :F_PAD] + y[:, F_PAD:2 * F_PAD]) + y[:, 2 * F_PAD:] + carry
        carry = c[CUMSUM_CHUNK - 1:CUMSUM_CHUNK, :]
        chunks.append(c)
    cum = jnp.concatenate(chunks, axis=0)
    carry_ref[...] = jnp.broadcast_to(carry, carry_ref.shape)

    cum1 = jnp.where(lane == N_HEADS, 1.0, cum)
    hi, mid, lo = _split3(cum1)
    aug = _dot(jnp.concatenate([hi, mid, lo], axis=1), place_ref[...])
    lane = lax.broadcasted_iota(jnp.int32, (tm, LANES), 1)
    for hd in range(N_HEADS):
        pair = slice((hd // 2) * LANES, (hd // 2 + 1) * LANES)
        own = (lane < HEAD_DIM) if hd % 2 == 0 else (lane >= HEAD_DIM)
        qa_ref[0, hd] = jnp.where(own, qn[:, pair], aug[:, hd * LANES:(hd + 1) * LANES]).astype(BF16)
        ka_ref[0, hd] = jnp.where(
            own, kn[:, pair], aug[:, (N_HEADS + hd) * LANES:(N_HEADS + hd + 1) * LANES]).astype(BF16)


def _placement_matrix():
    p = np.zeros((3 * F_PAD, 2 * N_HEADS * LANES), np.float32)
    for hd in range(N_HEADS):
        base = hd * LANES + (HEAD_DIM if hd % 2 == 0 else 0)
        kbase = N_HEADS * LANES + base
        for part in range(3):
            p[part * F_PAD + hd, base + part] = 1.0
            p[N_HEADS, base + 3 + part] = 1.0
            p[N_HEADS, kbase + part] = 1.0
            p[part * F_PAD + hd, kbase + 3 + part] = -1.0
    return p


def _in_proj(x, g, w_all, b_forget_pad, gq, gk):
    bsz, seq, _ = x.shape
    tm = ROW_TILE
    tri = jnp.asarray(np.tril(np.ones((CUMSUM_CHUNK, CUMSUM_CHUNK), np.float32)), BF16)
    place = jnp.asarray(_placement_matrix(), BF16)
    hsum = jnp.asarray(
        np.kron(np.eye(N_HEADS, dtype=np.float32), np.full((HEAD_DIM, HEAD_DIM), 1.0 / HEAD_DIM, np.float32)), BF16)
    row = lambda width: pl.BlockSpec((1, tm, width), lambda b, i: (b, i, 0))
    heads = pl.BlockSpec((1, N_HEADS, tm, LANES), lambda b, i: (b, 0, i, 0))
    return pl.pallas_call(
        _in_proj_kernel,
        grid=(bsz, seq // tm),
        in_specs=[row(D_MODEL), _const_spec((1, D_MODEL)), _const_spec((D_MODEL, IN_COLS_PADDED)),
                  _const_spec((1, F_PAD)), _const_spec((1, ATTN_WIDTH)), _const_spec((1, ATTN_WIDTH)),
                  _const_spec(tri.shape), _const_spec(place.shape), _const_spec(hsum.shape)],
        out_specs=[row(CONV_CH), heads, heads, row(ATTN_WIDTH), row(D_MODEL), row(D_MODEL)],
        out_shape=[jax.ShapeDtypeStruct((bsz, seq, CONV_CH), F32),
                   jax.ShapeDtypeStruct((bsz, N_HEADS, seq, LANES), BF16),
                   jax.ShapeDtypeStruct((bsz, N_HEADS, seq, LANES), BF16),
                   jax.ShapeDtypeStruct((bsz, seq, ATTN_WIDTH), BF16),
                   jax.ShapeDtypeStruct((bsz, seq, D_MODEL), BF16),
                   jax.ShapeDtypeStruct((bsz, seq, D_MODEL), BF16)],
        scratch_shapes=[pltpu.VMEM((8, F_PAD), F32)],
        compiler_params=pltpu.CompilerParams(
            dimension_semantics=("arbitrary", "arbitrary"), vmem_limit_bytes=VMEM_LIMIT_BYTES),
        name="in_proj",
    )(x, g, w_all, b_forget_pad, gq, gk, tri, place, hsum)


def _conv_kernel(a_ref, halo_ref, wdw_ref, bdw_ref, gln_ref, bln_ref, wout_ref, sga_ref, o_ref, win_ref, conv_ref):
    ts = a_ref.shape[1]
    first = pl.program_id(1) == 0
    win_ref[0, 0:HALO, :] = jnp.where(first, 0.0, halo_ref[0])
    win_ref[0, HALO:, :] = a_ref[0]
    shifted_rows = ts + HALO - SUBLANES
    for s in range(1, SUBLANES):
        win_ref[s, 0:shifted_rows, :] = win_ref[0, s:s + shifted_rows, :]
    lead = HALO - (CONV_WIDTH - 1)

    def rows(r, _):
        base = pl.multiple_of(r * CONV_ROWS, CONV_ROWS)
        acc = jnp.broadcast_to(bdw_ref[...], (CONV_ROWS, CONV_CH))
        for j in range(CONV_WIDTH):
            off = lead + j
            tap = win_ref[off % SUBLANES, pl.ds(base + (off // SUBLANES) * SUBLANES, CONV_ROWS), :]
            acc = acc + wdw_ref[j:j + 1, :] * tap
        conv_ref[pl.ds(base, CONV_ROWS), :] = acc
        return 0

    lax.fori_loop(0, ts // CONV_ROWS, rows, 0)

    c = conv_ref[...]
    mu = jnp.mean(c, axis=-1, keepdims=True)
    d = c - mu
    var = jnp.mean(d * d, axis=-1, keepdims=True)
    y = (d * lax.rsqrt(var + EPS)) * gln_ref[...] + bln_ref[...]
    y = y * _sigmoid(y)
    o_ref[0] = sga_ref[0].astype(F32) * _dot(y.astype(BF16), wout_ref[...])


def _conv_branch(a, w_dw, b_dw, g_ln, b_ln, w_out, sga):
    bsz, seq, _ = a.shape
    ts = ROW_TILE
    per_tile = ts // HALO
    return pl.pallas_call(
        _conv_kernel,
        grid=(bsz, seq // ts),
        in_specs=[pl.BlockSpec((1, ts, CONV_CH), lambda b, i: (b, i, 0)),
                  pl.BlockSpec((1, HALO, CONV_CH), lambda b, i: (b, jnp.maximum(i * per_tile - 1, 0), 0)),
                  _const_spec((HALO, CONV_CH)), _const_spec((1, CONV_CH)), _const_spec((1, CONV_CH)),
                  _const_spec((1, CONV_CH)), _const_spec((CONV_CH, D_MODEL)),
                  pl.BlockSpec((1, ts, D_MODEL), lambda b, i: (b, i, 0))],
        out_specs=pl.BlockSpec((1, ts, D_MODEL), lambda b, i: (b, i, 0)),
        out_shape=jax.ShapeDtypeStruct((bsz, seq, D_MODEL), F32),
        scratch_shapes=[pltpu.VMEM((SUBLANES, HALO + ts, CONV_CH), F32), pltpu.VMEM((ts, CONV_CH), F32)],
        compiler_params=pltpu.CompilerParams(
            dimension_semantics=("parallel", "parallel"), vmem_limit_bytes=VMEM_LIMIT_BYTES),
        name="conv_branch",
    )(a, a, w_dw, b_dw, g_ln, b_ln, w_out, sga)


def _attn_kernel(qa_ref, ka_ref, v_ref, o_ref, vext_ref, m_ref, acc_ref):
    tq = qa_ref.shape[2]
    tk = KV_TILE
    qi = pl.program_id(2)

    @pl.when(qi == 0)
    def _():
        vext_ref[:, :LANES] = v_ref[0]
        vext_ref[:, LANES:] = jnp.ones((vext_ref.shape[0], LANES), BF16)

    m_ref[...] = jnp.full(m_ref.shape, -jnp.inf, F32)
    acc_ref[...] = jnp.zeros_like(acc_ref)

    def step(j, masked):
        start = pl.multiple_of(j * tk, tk)
        vext = vext_ref[pl.ds(start, tk), :]
        for e in range(2):
            k = ka_ref[0, e, pl.ds(start, tk), :]
            s = lax.dot_general(qa_ref[0, e], k, (((1,), (1,)), ((), ())), preferred_element_type=F32)
            if masked:
                row = lax.broadcasted_iota(jnp.int32, (tq, tk), 0)
                col = lax.broadcasted_iota(jnp.int32, (tq, tk), 1)
                s = jnp.where(col <= row, s, -jnp.inf)
            m_prev = m_ref[e]
            m_new = jnp.maximum(m_prev, jnp.max(s, axis=1, keepdims=True))
            alpha = jnp.exp(m_prev - m_new)
            p = jnp.exp(s - jnp.concatenate([m_new] * (tk // LANES), axis=1))
            pv = _dot(p.astype(BF16), vext)
            acc_ref[e] = jnp.concatenate([alpha, alpha], axis=1) * acc_ref[e] + pv
            m_ref[e] = m_new

    def body(j, carry):
        step(j, masked=False)
        return carry

    lax.fori_loop(0, qi, body, 0)
    step(qi, masked=True)

    o0 = acc_ref[0, :, :LANES] / acc_ref[0, :, LANES:]
    o1 = acc_ref[1, :, :LANES] / acc_ref[1, :, LANES:]
    lane = lax.broadcasted_iota(jnp.int32, (tq, LANES), 1)
    o_ref[0] = jnp.where(lane < HEAD_DIM, o0, o1).astype(BF16)


def _attention(qa, ka, v):
    bsz, _, seq, _ = qa.shape
    assert Q_TILE == KV_TILE
    tq = Q_TILE
    return pl.pallas_call(
        _attn_kernel,
        grid=(bsz, N_HEADS // 2, seq // tq),
        in_specs=[pl.BlockSpec((1, 2, tq, LANES), lambda b, p, i: (b, p, i, 0)),
                  pl.BlockSpec((1, 2, seq, LANES), lambda b, p, i: (b, p, 0, 0)),
                  pl.BlockSpec((1, seq, LANES), lambda b, p, i: (b, 0, p))],
        out_specs=pl.BlockSpec((1, tq, LANES), lambda b, p, i: (b, i, p)),
        out_shape=jax.ShapeDtypeStruct((bsz, seq, ATTN_WIDTH), BF16),
        scratch_shapes=[pltpu.VMEM((seq, 2 * LANES), BF16),
                        pltpu.VMEM((2, tq, LANES), F32),
                        pltpu.VMEM((2, tq, 2 * LANES), F32)],
        compiler_params=pltpu.CompilerParams(
            dimension_semantics=("arbitrary", "arbitrary", "arbitrary"), vmem_limit_bytes=VMEM_LIMIT_BYTES),
        name="fox_attention",
    )(qa, ka, v)


def _mix_out_kernel(x_ref, gc_ref, sgb_ref, attn_ref, wao_ref, wo_ref, o_ref):
    y_attn = _dot(attn_ref[...], wao_ref[...])
    merged = gc_ref[...] + sgb_ref[...].astype(F32) * y_attn
    o_ref[...] = x_ref[...] + _dot(merged.astype(BF16), wo_ref[...])


def _mix_out(x, gc, sgb, attn, w_ao, w_o):
    t = x.shape[0]
    tm = ROW_TILE
    row = lambda width: pl.BlockSpec((tm, width), lambda i: (i, 0))
    return pl.pallas_call(
        _mix_out_kernel,
        grid=(t // tm,),
        in_specs=[row(D_MODEL), row(D_MODEL), row(D_MODEL), row(ATTN_WIDTH),
                  _const_spec((ATTN_WIDTH, D_MODEL)), _const_spec((D_MODEL, D_MODEL))],
        out_specs=row(D_MODEL),
        out_shape=jax.ShapeDtypeStruct((t, D_MODEL), F32),
        compiler_params=pltpu.CompilerParams(
            dimension_semantics=("parallel",), vmem_limit_bytes=VMEM_LIMIT_BYTES),
        name="mix_out",
    )(x, gc, sgb, attn, w_ao, w_o)


def _ffn_kernel(x_ref, g_ref, wi_ref, wo_ref, o_ref, acc_ref):
    x = x_ref[...]
    inv = lax.rsqrt(jnp.mean(x * x, axis=-1, keepdims=True) + EPS)
    h = ((x * inv) * g_ref[...]).astype(BF16)
    c0 = 0
    for n, width in enumerate(FFN_CHUNKS):
        gate = _dot(h, wi_ref[:, c0:c0 + width])
        up = _dot(h, wi_ref[:, D_FF + c0:D_FF + c0 + width])
        act = ((gate * _sigmoid(gate)) * up).astype(BF16)
        part = _dot(act, wo_ref[c0:c0 + width, :])
        if n == 0:
            acc_ref[...] = part
        else:
            acc_ref[...] += part
        c0 += width
    o_ref[...] = x + acc_ref[...]


def _ffn(x, g, w_i, w_o):
    t = x.shape[0]
    tm = ROW_TILE
    row = pl.BlockSpec((tm, D_MODEL), lambda i: (i, 0))
    return pl.pallas_call(
        _ffn_kernel,
        grid=(t // tm,),
        in_specs=[row, _const_spec((1, D_MODEL)), _const_spec((D_MODEL, 2 * D_FF)), _const_spec((D_FF, D_MODEL))],
        out_specs=row,
        out_shape=jax.ShapeDtypeStruct((t, D_MODEL), F32),
        scratch_shapes=[pltpu.VMEM((tm, D_MODEL), F32)],
        compiler_params=pltpu.CompilerParams(
            dimension_semantics=("parallel",), vmem_limit_bytes=VMEM_LIMIT_BYTES),
        name="ffn",
    )(x, g, w_i, w_o)


def kernel(x, g_mix, w_in, b_forget, w_dw, b_dw, g_conv_ln, b_conv_ln, w_conv_out,
           g_q, g_k, w_attn_out, w_out, g_ffn, w_ffn_in, w_ffn_out):
    bsz, seq, d = x.shape
    depth = w_in.shape[0]
    f0 = 2 * CONV_CH + 3 * ATTN_WIDTH
    w_all = jnp.concatenate(
        [w_in[:, :, :f0], w_in[:, :, f0 + N_HEADS:],
         jnp.pad(w_in[:, :, f0:f0 + N_HEADS], ((0, 0), (0, 0), (0, F_PAD - N_HEADS)))], axis=2).astype(BF16)
    bf_pad = jnp.pad(b_forget, ((0, 0), (0, F_PAD - N_HEADS)))
    scale = HEAD_DIM ** -0.5
    gq = jnp.tile(g_q, (1, N_HEADS)) * scale
    gk = jnp.tile(g_k, (1, N_HEADS))
    w_dw_pad = jnp.pad(w_dw, ((0, 0), (0, HALO - CONV_WIDTH), (0, 0)))

    for l in range(depth):
        a, qa, ka, v, sga, sgb = _in_proj(
            x, g_mix[l][None], w_all[l], bf_pad[l][None], gq[l][None], gk[l][None])
        gc = _conv_branch(a, w_dw_pad[l], b_dw[l][None], g_conv_ln[l][None], b_conv_ln[l][None],
                          w_conv_out[l].astype(BF16), sga)
        attn = _attention(qa, ka, v)
        x1 = _mix_out(x.reshape(bsz * seq, d), gc.reshape(bsz * seq, d), sgb.reshape(bsz * seq, d),
                      attn.reshape(bsz * seq, ATTN_WIDTH), w_attn_out[l].astype(BF16), w_out[l].astype(BF16))
        x = _ffn(x1, g_ffn[l][None], w_ffn_in[l].astype(BF16), w_ffn_out[l].astype(BF16)).reshape(bsz, seq, d)
    return x
```

```python
import functools

import jax
import jax.numpy as jnp
import numpy as np
from jax import lax
from jax.experimental import pallas as pl
from jax.experimental.pallas import tpu as pltpu

D_MODEL = 1024
CONV_CH = 512
CONV_WIDTH = 31
N_HEADS = 8
HEAD_DIM = 64
ATTN_WIDTH = N_HEADS * HEAD_DIM
D_FF = 2816
EPS = 1e-6
LOG2E = 1.4426950408889634

LANES = 128
SUBLANES = 8
HALO = 32
F_PAD = LANES
IN_COLS_PADDED = 2 * CONV_CH + 3 * ATTN_WIDTH + 2 * D_MODEL + F_PAD
VMEM_LIMIT_BYTES = 56 * 1024 * 1024

ROW_TILE = 512
CUMSUM_CHUNK = 256
CONV_ROWS = 32
Q_TILE = 512
KV_TILE = 512
FFN_CHUNKS = (1024, 1024, 768)

BF16 = jnp.bfloat16
F32 = jnp.float32


def _dot(a, b):
    return jnp.dot(a, b, preferred_element_type=F32)


def _split3(x):
    hi = x.astype(BF16)
    r = x - hi.astype(F32)
    mid = r.astype(BF16)
    lo = (r - mid.astype(F32)).astype(BF16)
    return hi, mid, lo


def _sigmoid(x):
    return 1.0 / (1.0 + jnp.exp(-x))


def _const_spec(shape):
    nd = len(shape)
    return pl.BlockSpec(shape, lambda *_: (0,) * nd, pipeline_mode=pl.Buffered(1))


def _in_proj_kernel(x_ref, g_ref, w_ref, bf_ref, gq_ref, gk_ref, tri_ref, place_ref, hsum_ref,
                    a_ref, qa_ref, ka_ref, v_ref, sga_ref, sgb_ref, carry_ref):
    tm = x_ref.shape[1]

    @pl.when(pl.program_id(1) == 0)
    def _():
        carry_ref[...] = jnp.zeros_like(carry_ref)

    x = x_ref[0]
    inv = lax.rsqrt(jnp.mean(x * x, axis=-1, keepdims=True) + EPS)
    h = ((x * inv) * g_ref[...]).astype(BF16)

    c0 = 0
    pa = _dot(h, w_ref[:, c0:c0 + 2 * CONV_CH])
    a_ref[0] = pa[:, :CONV_CH] * _sigmoid(pa[:, CONV_CH:])
    c0 += 2 * CONV_CH

    def head_rmsnorm(p, gain):
        sq = p * p
        hi = sq.astype(BF16)
        lo = (sq - hi.astype(F32)).astype(BF16)
        ms = _dot(hi, hsum_ref[...]) + _dot(lo, hsum_ref[...])
        return (p * lax.rsqrt(ms + EPS)) * gain

    qn = head_rmsnorm(_dot(h, w_ref[:, c0:c0 + ATTN_WIDTH]), gq_ref[...])
    c0 += ATTN_WIDTH
    kn = head_rmsnorm(_dot(h, w_ref[:, c0:c0 + ATTN_WIDTH]), gk_ref[...])
    c0 += ATTN_WIDTH
    v_ref[0] = _dot(h, w_ref[:, c0:c0 + ATTN_WIDTH]).astype(BF16)
    c0 += ATTN_WIDTH
    sga_ref[0] = _sigmoid(_dot(h, w_ref[:, c0:c0 + D_MODEL])).astype(BF16)
    c0 += D_MODEL
    sgb_ref[0] = _sigmoid(_dot(h, w_ref[:, c0:c0 + D_MODEL])).astype(BF16)
    c0 += D_MODEL

    z = _dot(h, w_ref[:, c0:c0 + F_PAD]) + bf_ref[...]
    lane = lax.broadcasted_iota(jnp.int32, (tm, F_PAD), 1)
    logf = jnp.where(lane < N_HEADS, jnp.minimum(z, 0.0) - jnp.log1p(jnp.exp(-jnp.abs(z))), 0.0)
    hi, mid, lo = _split3(logf)
    parts = jnp.concatenate([hi, mid, lo], axis=1)
    carry = carry_ref[0:1, :]
    chunks = []
    for ch in range(tm // CUMSUM_CHUNK):
        y = _dot(tri_ref[...], parts[ch * CUMSUM_CHUNK:(ch + 1) * CUMSUM_CHUNK])
        c = (y[:, :F_PAD] + y[:, F_PAD:2 * F_PAD]) + y[:, 2 * F_PAD:] + carry
        carry = c[CUMSUM_CHUNK - 1:CUMSUM_CHUNK, :]
        chunks.append(c)
    cum = jnp.concatenate(chunks, axis=0)
    carry_ref[...] = jnp.broadcast_to(carry, carry_ref.shape)

    cum1 = jnp.where(lane == N_HEADS, 1.0, cum * LOG2E)
    hi, mid, lo = _split3(cum1)
    aug = _dot(jnp.concatenate([hi, mid, lo], axis=1), place_ref[...])
    lane = lax.broadcasted_iota(jnp.int32, (tm, LANES), 1)
    for hd in range(N_HEADS):
        pair = slice((hd // 2) * LANES, (hd // 2 + 1) * LANES)
        own = (lane < HEAD_DIM) if hd % 2 == 0 else (lane >= HEAD_DIM)
        qa_ref[0, hd] = jnp.where(own, qn[:, pair], aug[:, hd * LANES:(hd + 1) * LANES]).astype(BF16)
        ka_ref[0, hd] = jnp.where(
            own, kn[:, pair], aug[:, (N_HEADS + hd) * LANES:(N_HEADS + hd + 1) * LANES]).astype(BF16)


def _placement_matrix():
    p = np.zeros((3 * F_PAD, 2 * N_HEADS * LANES), np.float32)
    for hd in range(N_HEADS):
        base = hd * LANES + (HEAD_DIM if hd % 2 == 0 else 0)
        kbase = N_HEADS * LANES + base
        for part in range(3):
            p[part * F_PAD + hd, base + part] = 1.0
            p[N_HEADS, base + 3 + part] = 1.0
            p[N_HEADS, kbase + part] = 1.0
            p[part * F_PAD + hd, kbase + 3 + part] = -1.0
    return p


def _in_proj(x, g, w_all, b_forget_pad, gq, gk):
    bsz, seq, _ = x.shape
    tm = ROW_TILE
    tri = jnp.asarray(np.tril(np.ones((CUMSUM_CHUNK, CUMSUM_CHUNK), np.float32)), BF16)
    place = jnp.asarray(_placement_matrix(), BF16)
    hsum = jnp.asarray(
        np.kron(np.eye(N_HEADS, dtype=np.float32), np.full((HEAD_DIM, HEAD_DIM), 1.0 / HEAD_DIM, np.float32)), BF16)
    row = lambda width: pl.BlockSpec((1, tm, width), lambda b, i: (b, i, 0))
    heads = pl.BlockSpec((1, N_HEADS, tm, LANES), lambda b, i: (b, 0, i, 0))
    return pl.pallas_call(
        _in_proj_kernel,
        grid=(bsz, seq // tm),
        in_specs=[row(D_MODEL), _const_spec((1, D_MODEL)), _const_spec((D_MODEL, IN_COLS_PADDED)),
                  _const_spec((1, F_PAD)), _const_spec((1, ATTN_WIDTH)), _const_spec((1, ATTN_WIDTH)),
                  _const_spec(tri.shape), _const_spec(place.shape), _const_spec(hsum.shape)],
        out_specs=[row(CONV_CH), heads, heads, row(ATTN_WIDTH), row(D_MODEL), row(D_MODEL)],
        out_shape=[jax.ShapeDtypeStruct((bsz, seq, CONV_CH), F32),
                   jax.ShapeDtypeStruct((bsz, N_HEADS, seq, LANES), BF16),
                   jax.ShapeDtypeStruct((bsz, N_HEADS, seq, LANES), BF16),
                   jax.ShapeDtypeStruct((bsz, seq, ATTN_WIDTH), BF16),
                   jax.ShapeDtypeStruct((bsz, seq, D_MODEL), BF16),
                   jax.ShapeDtypeStruct((bsz, seq, D_MODEL), BF16)],
        scratch_shapes=[pltpu.VMEM((8, F_PAD), F32)],
        compiler_params=pltpu.CompilerParams(
            dimension_semantics=("arbitrary", "arbitrary"), vmem_limit_bytes=VMEM_LIMIT_BYTES),
        name="in_proj",
    )(x, g, w_all, b_forget_pad, gq, gk, tri, place, hsum)


def _conv_kernel(a_ref, halo_ref, wdw_ref, bdw_ref, gln_ref, bln_ref, wout_ref, sga_ref, o_ref, win_ref, conv_ref):
    ts = a_ref.shape[1]
    first = pl.program_id(1) == 0
    win_ref[0, 0:HALO, :] = jnp.where(first, 0.0, halo_ref[0])
    win_ref[0, HALO:, :] = a_ref[0]
    shifted_rows = ts + HALO - SUBLANES
    for s in range(1, SUBLANES):
        win_ref[s, 0:shifted_rows, :] = win_ref[0, s:s + shifted_rows, :]
    lead = HALO - (CONV_WIDTH - 1)

    def rows(r, _):
        base = pl.multiple_of(r * CONV_ROWS, CONV_ROWS)
        acc = jnp.broadcast_to(bdw_ref[...], (CONV_ROWS, CONV_CH))
        for j in range(CONV_WIDTH):
            off = lead + j
            tap = win_ref[off % SUBLANES, pl.ds(base + (off // SUBLANES) * SUBLANES, CONV_ROWS), :]
            acc = acc + wdw_ref[j:j + 1, :] * tap
        conv_ref[pl.ds(base, CONV_ROWS), :] = acc
        return 0

    lax.fori_loop(0, ts // CONV_ROWS, rows, 0)

    c = conv_ref[...]
    mu = jnp.mean(c, axis=-1, keepdims=True)
    d = c - mu
    var = jnp.mean(d * d, axis=-1, keepdims=True)
    y = (d * lax.rsqrt(var + EPS)) * gln_ref[...] + bln_ref[...]
    y = y * _sigmoid(y)
    o_ref[0] = sga_ref[0].astype(F32) * _dot(y.astype(BF16), wout_ref[...])


def _conv_branch(a, w_dw, b_dw, g_ln, b_ln, w_out, sga):
    bsz, seq, _ = a.shape
    ts = ROW_TILE
    per_tile = ts // HALO
    return pl.pallas_call(
        _conv_kernel,
        grid=(bsz, seq // ts),
        in_specs=[pl.BlockSpec((1, ts, CONV_CH), lambda b, i: (b, i, 0)),
                  pl.BlockSpec((1, HALO, CONV_CH), lambda b, i: (b, jnp.maximum(i * per_tile - 1, 0), 0)),
                  _const_spec((HALO, CONV_CH)), _const_spec((1, CONV_CH)), _const_spec((1, CONV_CH)),
                  _const_spec((1, CONV_CH)), _const_spec((CONV_CH, D_MODEL)),
                  pl.BlockSpec((1, ts, D_MODEL), lambda b, i: (b, i, 0))],
        out_specs=pl.BlockSpec((1, ts, D_MODEL), lambda b, i: (b, i, 0)),
        out_shape=jax.ShapeDtypeStruct((bsz, seq, D_MODEL), F32),
        scratch_shapes=[pltpu.VMEM((SUBLANES, HALO + ts, CONV_CH), F32), pltpu.VMEM((ts, CONV_CH), F32)],
        compiler_params=pltpu.CompilerParams(
            dimension_semantics=("parallel", "parallel"), vmem_limit_bytes=VMEM_LIMIT_BYTES),
        name="conv_branch",
    )(a, a, w_dw, b_dw, g_ln, b_ln, w_out, sga)


def _attn_kernel(qa_ref, ka_ref, v_ref, o_ref, vext_ref, m_ref, acc_ref, sa_ref, sb_ref):
    tq = qa_ref.shape[2]
    tk = KV_TILE
    qi = pl.program_id(2)

    @pl.when(qi == 0)
    def _():
        vext_ref[:, :LANES] = v_ref[0]
        vext_ref[:, LANES:] = jnp.ones((vext_ref.shape[0], LANES), BF16)

    m_ref[...] = jnp.full(m_ref.shape, -jnp.inf, F32)
    acc_ref[...] = jnp.zeros_like(acc_ref)

    def scores(j, s_ref):
        start = pl.multiple_of(j * tk, tk)
        for e in range(2):
            k = ka_ref[0, e, pl.ds(start, tk), :]
            s_ref[e] = lax.dot_general(qa_ref[0, e], k, (((1,), (1,)), ((), ())), preferred_element_type=F32)

    def update(j, s_ref, masked):
        start = pl.multiple_of(j * tk, tk)
        vext = vext_ref[pl.ds(start, tk), :]
        for e in range(2):
            s = s_ref[e]
            if masked:
                row = lax.broadcasted_iota(jnp.int32, (tq, tk), 0)
                col = lax.broadcasted_iota(jnp.int32, (tq, tk), 1)
                s = jnp.where(col <= row, s, -jnp.inf)
            m_prev = m_ref[e]
            m_new = jnp.maximum(m_prev, jnp.max(s, axis=1, keepdims=True))
            alpha = jnp.exp2(m_prev - m_new)
            p = jnp.exp2(s - jnp.concatenate([m_new] * (tk // LANES), axis=1))
            pv = _dot(p.astype(BF16), vext)
            acc_ref[e] = jnp.concatenate([alpha, alpha], axis=1) * acc_ref[e] + pv
            m_ref[e] = m_new

    scores(0, sa_ref)

    def pair(t, carry):
        j = 2 * t
        scores(j + 1, sb_ref)
        update(j, sa_ref, masked=False)
        scores(j + 2, sa_ref)
        update(j + 1, sb_ref, masked=False)
        return carry

    lax.fori_loop(0, qi // 2, pair, 0)

    @pl.when(qi % 2 == 1)
    def _():
        scores(qi, sb_ref)
        update(qi - 1, sa_ref, masked=False)
        update(qi, sb_ref, masked=True)

    @pl.when(qi % 2 == 0)
    def _():
        update(qi, sa_ref, masked=True)

    o0 = acc_ref[0, :, :LANES] / acc_ref[0, :, LANES:]
    o1 = acc_ref[1, :, :LANES] / acc_ref[1, :, LANES:]
    lane = lax.broadcasted_iota(jnp.int32, (tq, LANES), 1)
    o_ref[0] = jnp.where(lane < HEAD_DIM, o0, o1).astype(BF16)


def _attention(qa, ka, v):
    bsz, _, seq, _ = qa.shape
    assert Q_TILE == KV_TILE
    tq = Q_TILE
    return pl.pallas_call(
        _attn_kernel,
        grid=(bsz, N_HEADS // 2, seq // tq),
        in_specs=[pl.BlockSpec((1, 2, tq, LANES), lambda b, p, i: (b, p, i, 0)),
                  pl.BlockSpec((1, 2, seq, LANES), lambda b, p, i: (b, p, 0, 0)),
                  pl.BlockSpec((1, seq, LANES), lambda b, p, i: (b, 0, p))],
        out_specs=pl.BlockSpec((1, tq, LANES), lambda b, p, i: (b, i, p)),
        out_shape=jax.ShapeDtypeStruct((bsz, seq, ATTN_WIDTH), BF16),
        scratch_shapes=[pltpu.VMEM((seq, 2 * LANES), BF16),
                        pltpu.VMEM((2, tq, LANES), F32),
                        pltpu.VMEM((2, tq, 2 * LANES), F32),
                        pltpu.VMEM((2, tq, KV_TILE), F32),
                        pltpu.VMEM((2, tq, KV_TILE), F32)],
        compiler_params=pltpu.CompilerParams(
            dimension_semantics=("arbitrary", "arbitrary", "arbitrary"), vmem_limit_bytes=VMEM_LIMIT_BYTES),
        name="fox_attention",
    )(qa, ka, v)


def _mix_out_kernel(x_ref, gc_ref, sgb_ref, attn_ref, wao_ref, wo_ref, o_ref):
    y_attn = _dot(attn_ref[...], wao_ref[...])
    merged = gc_ref[...] + sgb_ref[...].astype(F32) * y_attn
    o_ref[...] = x_ref[...] + _dot(merged.astype(BF16), wo_ref[...])


def _mix_out(x, gc, sgb, attn, w_ao, w_o):
    t = x.shape[0]
    tm = ROW_TILE
    row = lambda width: pl.BlockSpec((tm, width), lambda i: (i, 0))
    return pl.pallas_call(
        _mix_out_kernel,
        grid=(t // tm,),
        in_specs=[row(D_MODEL), row(D_MODEL), row(D_MODEL), row(ATTN_WIDTH),
                  _const_spec((ATTN_WIDTH, D_MODEL)), _const_spec((D_MODEL, D_MODEL))],
        out_specs=row(D_MODEL),
        out_shape=jax.ShapeDtypeStruct((t, D_MODEL), F32),
        compiler_params=pltpu.CompilerParams(
            dimension_semantics=("parallel",), vmem_limit_bytes=VMEM_LIMIT_BYTES),
        name="mix_out",
    )(x, gc, sgb, attn, w_ao, w_o)


def _ffn_kernel(x_ref, g_ref, wi_ref, wo_ref, o_ref, acc_ref):
    x = x_ref[...]
    inv = lax.rsqrt(jnp.mean(x * x, axis=-1, keepdims=True) + EPS)
    h = ((x * inv) * g_ref[...]).astype(BF16)
    c0 = 0
    for n, width in enumerate(FFN_CHUNKS):
        gate = _dot(h, wi_ref[:, c0:c0 + width])
        up = _dot(h, wi_ref[:, D_FF + c0:D_FF + c0 + width])
        act = ((gate * _sigmoid(gate)) * up).astype(BF16)
        part = _dot(act, wo_ref[c0:c0 + width, :])
        if n == 0:
            acc_ref[...] = part
        else:
            acc_ref[...] += part
        c0 += width
    o_ref[...] = x + acc_ref[...]


def _ffn(x, g, w_i, w_o):
    t = x.shape[0]
    tm = ROW_TILE
    row = pl.BlockSpec((tm, D_MODEL), lambda i: (i, 0))
    return pl.pallas_call(
        _ffn_kernel,
        grid=(t // tm,),
        in_specs=[row, _const_spec((1, D_MODEL)), _const_spec((D_MODEL, 2 * D_FF)), _const_spec((D_FF, D_MODEL))],
        out_specs=row,
        out_shape=jax.ShapeDtypeStruct((t, D_MODEL), F32),
        scratch_shapes=[pltpu.VMEM((tm, D_MODEL), F32)],
        compiler_params=pltpu.CompilerParams(
            dimension_semantics=("parallel",), vmem_limit_bytes=VMEM_LIMIT_BYTES),
        name="ffn",
    )(x, g, w_i, w_o)


def kernel(x, g_mix, w_in, b_forget, w_dw, b_dw, g_conv_ln, b_conv_ln, w_conv_out,
           g_q, g_k, w_attn_out, w_out, g_ffn, w_ffn_in, w_ffn_out):
    bsz, seq, d = x.shape
    depth = w_in.shape[0]
    f0 = 2 * CONV_CH + 3 * ATTN_WIDTH
    w_all = jnp.concatenate(
        [w_in[:, :, :f0], w_in[:, :, f0 + N_HEADS:],
         jnp.pad(w_in[:, :, f0:f0 + N_HEADS], ((0, 0), (0, 0), (0, F_PAD - N_HEADS)))], axis=2).astype(BF16)
    bf_pad = jnp.pad(b_forget, ((0, 0), (0, F_PAD - N_HEADS)))
    scale = HEAD_DIM ** -0.5 * LOG2E
    gq = jnp.tile(g_q, (1, N_HEADS)) * scale
    gk = jnp.tile(g_k, (1, N_HEADS))
    w_dw_pad = jnp.pad(w_dw, ((0, 0), (0, HALO - CONV_WIDTH), (0, 0)))

    for l in range(depth):
        a, qa, ka, v, sga, sgb = _in_proj(
            x, g_mix[l][None], w_all[l], bf_pad[l][None], gq[l][None], gk[l][None])
        gc = _conv_branch(a, w_dw_pad[l], b_dw[l][None], g_conv_ln[l][None], b_conv_ln[l][None],
                          w_conv_out[l].astype(BF16), sga)
        attn = _attention(qa, ka, v)
        x1 = _mix_out(x.reshape(bsz * seq, d), gc.reshape(bsz * seq, d), sgb.reshape(bsz * seq, d),
                      attn.reshape(bsz * seq, ATTN_WIDTH), w_attn_out[l].astype(BF16), w_out[l].astype(BF16))
        x = _ffn(x1, g_ffn[l][None], w_ffn_in[l].astype(BF16), w_ffn_out[l].astype(BF16)).reshape(bsz, seq, d)
    return x
```

```python
import jax
import jax.numpy as jnp
import numpy as np
from jax import lax
from jax.experimental import pallas as pl
from jax.experimental.pallas import tpu as pltpu

D_MODEL = 1024
CONV_CH = 512
CONV_WIDTH = 31
N_HEADS = 8
HEAD_DIM = 64
ATTN_WIDTH = N_HEADS * HEAD_DIM
D_FF = 2816
EPS = 1e-6
LOG2E = 1.4426950408889634

LANES = 128
SUBLANES = 8
HALO = 32
F_PAD = LANES
F_COPIES = 3
F_STRIDE = 16
ONES_LANE = N_HEADS
IN_COLS_PADDED = 2 * CONV_CH + 3 * ATTN_WIDTH + 2 * D_MODEL + F_PAD
VMEM_LIMIT_BYTES = 56 * 1024 * 1024

ROW_TILE = 512
CUMSUM_CHUNK = 256
CONV_ROWS = 32
Q_TILE = 512
KV_TILE = 512
FFN_CHUNKS = (1024, 1024, 768)

BF16 = jnp.bfloat16
F32 = jnp.float32


def _dot(a, b):
    return jnp.dot(a, b, preferred_element_type=F32)


def _split3(x):
    hi = x.astype(BF16)
    r = x - hi.astype(F32)
    mid = r.astype(BF16)
    lo = (r - mid.astype(F32)).astype(BF16)
    return hi, mid, lo


def _sigmoid(x):
    return 1.0 / (1.0 + jnp.exp(-x))


def _const_spec(shape):
    nd = len(shape)
    return pl.BlockSpec(shape, lambda *_: (0,) * nd, pipeline_mode=pl.Buffered(1))


def _layer_spec(shape, layer):
    nd = len(shape)
    return pl.BlockSpec((1,) + tuple(shape), lambda *_: (layer,) + (0,) * nd, pipeline_mode=pl.Buffered(1))


def _head_rmsnorm(p, gain):
    rows = p.shape[0]
    low = lax.broadcasted_iota(jnp.int32, (rows, LANES), 1) < HEAD_DIM
    out = []
    for g in range(p.shape[1] // LANES):
        blk = p[:, g * LANES:(g + 1) * LANES]
        sq = blk * blk
        s_low = jnp.sum(jnp.where(low, sq, 0.0), axis=1, keepdims=True)
        s_high = jnp.sum(jnp.where(low, 0.0, sq), axis=1, keepdims=True)
        ms = jnp.where(low, s_low, s_high) * (1.0 / HEAD_DIM)
        out.append(blk * lax.rsqrt(ms + EPS))
    return jnp.concatenate(out, axis=1) * gain


def _in_proj_kernel(x_ref, g_ref, w_ref, bf_ref, gq_ref, gk_ref, tri_ref, place_ref,
                    a_ref, qa_ref, ka_ref, v_ref, sga_ref, sgb_ref, carry_ref):
    tm = x_ref.shape[1]

    @pl.when(pl.program_id(1) == 0)
    def _():
        carry_ref[...] = jnp.zeros_like(carry_ref)

    x = x_ref[0]
    inv = lax.rsqrt(jnp.mean(x * x, axis=-1, keepdims=True) + EPS)
    h = ((x * inv) * g_ref[0]).astype(BF16)

    c0 = 0
    pa = _dot(h, w_ref[0, :, c0:c0 + 2 * CONV_CH])
    a_ref[0] = pa[:, :CONV_CH] * _sigmoid(pa[:, CONV_CH:])
    c0 += 2 * CONV_CH
    qn = _head_rmsnorm(_dot(h, w_ref[0, :, c0:c0 + ATTN_WIDTH]), gq_ref[0])
    c0 += ATTN_WIDTH
    kn = _head_rmsnorm(_dot(h, w_ref[0, :, c0:c0 + ATTN_WIDTH]), gk_ref[0])
    c0 += ATTN_WIDTH
    v_ref[0] = _dot(h, w_ref[0, :, c0:c0 + ATTN_WIDTH]).astype(BF16)
    c0 += ATTN_WIDTH
    sga_ref[0] = _sigmoid(_dot(h, w_ref[0, :, c0:c0 + D_MODEL])).astype(BF16)
    c0 += D_MODEL
    sgb_ref[0] = _sigmoid(_dot(h, w_ref[0, :, c0:c0 + D_MODEL])).astype(BF16)
    c0 += D_MODEL

    z = _dot(h, w_ref[0, :, c0:c0 + F_PAD]) + bf_ref[0]
    lane = lax.broadcasted_iota(jnp.int32, (tm, F_PAD), 1)
    valid = (lane < F_COPIES * F_STRIDE) & ((lane & N_HEADS) == 0)
    logf = jnp.where(valid, jnp.minimum(z, 0.0) - jnp.log1p(jnp.exp(-jnp.abs(z))), 0.0)
    hi, mid, lo = _split3(logf)
    parts = jnp.concatenate([hi, mid, lo], axis=1)
    carry = carry_ref[0:1, :]
    chunks = []
    for ch in range(tm // CUMSUM_CHUNK):
        y = _dot(tri_ref[...], parts[ch * CUMSUM_CHUNK:(ch + 1) * CUMSUM_CHUNK])
        c = (y[:, :F_PAD] + y[:, F_PAD:2 * F_PAD]) + y[:, 2 * F_PAD:] + carry
        carry = c[CUMSUM_CHUNK - 1:CUMSUM_CHUNK, :]
        chunks.append(c)
    cum = jnp.concatenate(chunks, axis=0)
    carry_ref[...] = jnp.broadcast_to(carry, carry_ref.shape)

    cum1 = jnp.where(lane == ONES_LANE, 1.0, cum * LOG2E)
    hi, mid, lo = _split3(cum1)
    packed = jnp.where(lane < F_STRIDE, hi.astype(F32),
                       jnp.where(lane < 2 * F_STRIDE, mid.astype(F32), lo.astype(F32))).astype(BF16)
    aug = _dot(packed, place_ref[...])
    low = lax.broadcasted_iota(jnp.int32, (tm, LANES), 1) < HEAD_DIM
    for pr in range(N_HEADS // 2):
        pair = slice(pr * LANES, (pr + 1) * LANES)
        kpair = slice(ATTN_WIDTH + pr * LANES, ATTN_WIDTH + (pr + 1) * LANES)
        qa_ref[0, 2 * pr] = jnp.where(low, qn[:, pair], aug[:, pair]).astype(BF16)
        qa_ref[0, 2 * pr + 1] = jnp.where(low, aug[:, pair], qn[:, pair]).astype(BF16)
        ka_ref[0, 2 * pr] = jnp.where(low, kn[:, pair], aug[:, kpair]).astype(BF16)
        ka_ref[0, 2 * pr + 1] = jnp.where(low, aug[:, kpair], kn[:, pair]).astype(BF16)


def _placement_matrix():
    p = np.zeros((F_PAD, 2 * ATTN_WIDTH), np.float32)
    for hd in range(N_HEADS):
        base = (hd // 2) * LANES + (HEAD_DIM if hd % 2 == 0 else 0)
        kbase = ATTN_WIDTH + base
        for part in range(F_COPIES):
            p[part * F_STRIDE + hd, base + part] = 1.0
            p[ONES_LANE, base + 3 + part] = 1.0
            p[ONES_LANE, kbase + part] = 1.0
            p[part * F_STRIDE + hd, kbase + 3 + part] = -1.0
    return p


def _in_proj(x, layer, g, w_all, b_forget_pad, gq, gk):
    bsz, seq, _ = x.shape
    tm = ROW_TILE
    tri = jnp.asarray(np.tril(np.ones((CUMSUM_CHUNK, CUMSUM_CHUNK), np.float32)), BF16)
    place = jnp.asarray(_placement_matrix(), BF16)
    row = lambda width: pl.BlockSpec((1, tm, width), lambda b, i: (b, i, 0))
    heads = pl.BlockSpec((1, N_HEADS, tm, LANES), lambda b, i: (b, 0, i, 0))
    return pl.pallas_call(
        _in_proj_kernel,
        grid=(bsz, seq // tm),
        in_specs=[row(D_MODEL), _layer_spec((1, D_MODEL), layer), _layer_spec((D_MODEL, IN_COLS_PADDED), layer),
                  _layer_spec((1, F_PAD), layer), _layer_spec((1, ATTN_WIDTH), layer),
                  _layer_spec((1, ATTN_WIDTH), layer), _const_spec(tri.shape), _const_spec(place.shape)],
        out_specs=[row(CONV_CH), heads, heads, row(ATTN_WIDTH), row(D_MODEL), row(D_MODEL)],
        out_shape=[jax.ShapeDtypeStruct((bsz, seq, CONV_CH), F32),
                   jax.ShapeDtypeStruct((bsz, N_HEADS, seq, LANES), BF16),
                   jax.ShapeDtypeStruct((bsz, N_HEADS, seq, LANES), BF16),
                   jax.ShapeDtypeStruct((bsz, seq, ATTN_WIDTH), BF16),
                   jax.ShapeDtypeStruct((bsz, seq, D_MODEL), BF16),
                   jax.ShapeDtypeStruct((bsz, seq, D_MODEL), BF16)],
        scratch_shapes=[pltpu.VMEM((SUBLANES, F_PAD), F32)],
        compiler_params=pltpu.CompilerParams(
            dimension_semantics=("arbitrary", "arbitrary"), vmem_limit_bytes=VMEM_LIMIT_BYTES),
        name="in_proj",
    )(x, g, w_all, b_forget_pad, gq, gk, tri, place)


def _conv_kernel(a_ref, halo_ref, wdw_ref, bdw_ref, gln_ref, bln_ref, wout_ref, sga_ref, o_ref, win_ref, conv_ref):
    ts = a_ref.shape[1]
    first = pl.program_id(1) == 0
    win_ref[0, 0:HALO, :] = jnp.where(first, 0.0, halo_ref[0])
    win_ref[0, HALO:, :] = a_ref[0]
    shifted_rows = ts + HALO - SUBLANES
    for s in range(1, SUBLANES):
        win_ref[s, 0:shifted_rows, :] = win_ref[0, s:s + shifted_rows, :]
    lead = HALO - (CONV_WIDTH - 1)

    def rows(r, _):
        base = pl.multiple_of(r * CONV_ROWS, CONV_ROWS)
        acc = jnp.broadcast_to(bdw_ref[0], (CONV_ROWS, CONV_CH))
        for j in range(CONV_WIDTH):
            off = lead + j
            tap = win_ref[off % SUBLANES, pl.ds(base + (off // SUBLANES) * SUBLANES, CONV_ROWS), :]
            acc = acc + wdw_ref[0, j:j + 1, :] * tap
        conv_ref[pl.ds(base, CONV_ROWS), :] = acc
        return 0

    lax.fori_loop(0, ts // CONV_ROWS, rows, 0)

    c = conv_ref[...]
    mu = jnp.mean(c, axis=-1, keepdims=True)
    d = c - mu
    var = jnp.mean(d * d, axis=-1, keepdims=True)
    y = (d * lax.rsqrt(var + EPS)) * gln_ref[0] + bln_ref[0]
    y = y * _sigmoid(y)
    o_ref[0] = sga_ref[0].astype(F32) * _dot(y.astype(BF16), wout_ref[0].astype(BF16))


def _conv_branch(a, layer, w_dw, b_dw, g_ln, b_ln, w_out, sga):
    bsz, seq, _ = a.shape
    ts = ROW_TILE
    per_tile = ts // HALO
    return pl.pallas_call(
        _conv_kernel,
        grid=(bsz, seq // ts),
        in_specs=[pl.BlockSpec((1, ts, CONV_CH), lambda b, i: (b, i, 0)),
                  pl.BlockSpec((1, HALO, CONV_CH), lambda b, i: (b, jnp.maximum(i * per_tile - 1, 0), 0)),
                  _layer_spec((HALO, CONV_CH), layer), _layer_spec((1, CONV_CH), layer),
                  _layer_spec((1, CONV_CH), layer), _layer_spec((1, CONV_CH), layer),
                  _layer_spec((CONV_CH, D_MODEL), layer),
                  pl.BlockSpec((1, ts, D_MODEL), lambda b, i: (b, i, 0))],
        out_specs=pl.BlockSpec((1, ts, D_MODEL), lambda b, i: (b, i, 0)),
        out_shape=jax.ShapeDtypeStruct((bsz, seq, D_MODEL), F32),
        scratch_shapes=[pltpu.VMEM((SUBLANES, HALO + ts, CONV_CH), F32), pltpu.VMEM((ts, CONV_CH), F32)],
        compiler_params=pltpu.CompilerParams(
            dimension_semantics=("parallel", "parallel"), vmem_limit_bytes=VMEM_LIMIT_BYTES),
        name="conv_branch",
    )(a, a, w_dw, b_dw, g_ln, b_ln, w_out, sga)


def _attn_kernel(qa_ref, ka_ref, v_ref, o_ref, vext_ref, m_ref, acc_ref, sa_ref, sb_ref):
    tq = qa_ref.shape[2]
    tk = KV_TILE
    qi = pl.program_id(2)

    @pl.when(qi == 0)
    def _():
        vext_ref[:, :LANES] = v_ref[0]
        vext_ref[:, LANES:] = jnp.ones((vext_ref.shape[0], LANES), BF16)

    m_ref[...] = jnp.full(m_ref.shape, -jnp.inf, F32)
    acc_ref[...] = jnp.zeros_like(acc_ref)

    def scores(j, s_ref):
        start = pl.multiple_of(j * tk, tk)
        for e in range(2):
            k = ka_ref[0, e, pl.ds(start, tk), :]
            s_ref[e] = lax.dot_general(qa_ref[0, e], k, (((1,), (1,)), ((), ())), preferred_element_type=F32)

    def update(j, s_ref, masked):
        start = pl.multiple_of(j * tk, tk)
        vext = vext_ref[pl.ds(start, tk), :]
        for e in range(2):
            s = s_ref[e]
            if masked:
                row = lax.broadcasted_iota(jnp.int32, (tq, tk), 0)
                col = lax.broadcasted_iota(jnp.int32, (tq, tk), 1)
                s = jnp.where(col <= row, s, -jnp.inf)
            m_prev = m_ref[e]
            m_new = jnp.maximum(m_prev, jnp.max(s, axis=1, keepdims=True))
            alpha = jnp.exp2(m_prev - m_new)
            p = jnp.exp2(s - jnp.concatenate([m_new] * (tk // LANES), axis=1))
            pv = _dot(p.astype(BF16), vext)
            acc_ref[e] = jnp.concatenate([alpha, alpha], axis=1) * acc_ref[e] + pv
            m_ref[e] = m_new

    scores(0, sa_ref)

    def pair(t, carry):
        j = 2 * t
        scores(j + 1, sb_ref)
        update(j, sa_ref, masked=False)
        scores(j + 2, sa_ref)
        update(j + 1, sb_ref, masked=False)
        return carry

    lax.fori_loop(0, qi // 2, pair, 0)

    @pl.when(qi % 2 == 1)
    def _():
        scores(qi, sb_ref)
        update(qi - 1, sa_ref, masked=False)
        update(qi, sb_ref, masked=True)

    @pl.when(qi % 2 == 0)
    def _():
        update(qi, sa_ref, masked=True)

    o0 = acc_ref[0, :, :LANES] / acc_ref[0, :, LANES:]
    o1 = acc_ref[1, :, :LANES] / acc_ref[1, :, LANES:]
    lane = lax.broadcasted_iota(jnp.int32, (tq, LANES), 1)
    o_ref[0] = jnp.where(lane < HEAD_DIM, o0, o1).astype(BF16)


def _attention(qa, ka, v):
    bsz, _, seq, _ = qa.shape
    assert Q_TILE == KV_TILE
    tq = Q_TILE
    return pl.pallas_call(
        _attn_kernel,
        grid=(bsz, N_HEADS // 2, seq // tq),
        in_specs=[pl.BlockSpec((1, 2, tq, LANES), lambda b, p, i: (b, p, i, 0)),
                  pl.BlockSpec((1, 2, seq, LANES), lambda b, p, i: (b, p, 0, 0)),
                  pl.BlockSpec((1, seq, LANES), lambda b, p, i: (b, 0, p))],
        out_specs=pl.BlockSpec((1, tq, LANES), lambda b, p, i: (b, i, p)),
        out_shape=jax.ShapeDtypeStruct((bsz, seq, ATTN_WIDTH), BF16),
        scratch_shapes=[pltpu.VMEM((seq, 2 * LANES), BF16),
                        pltpu.VMEM((2, tq, LANES), F32),
                        pltpu.VMEM((2, tq, 2 * LANES), F32),
                        pltpu.VMEM((2, tq, KV_TILE), F32),
                        pltpu.VMEM((2, tq, KV_TILE), F32)],
        compiler_params=pltpu.CompilerParams(
            dimension_semantics=("arbitrary", "arbitrary", "arbitrary"), vmem_limit_bytes=VMEM_LIMIT_BYTES),
        name="fox_attention",
    )(qa, ka, v)


def _mix_out_kernel(x_ref, gc_ref, sgb_ref, attn_ref, wao_ref, wo_ref, o_ref):
    y_attn = _dot(attn_ref[...], wao_ref[0].astype(BF16))
    merged = gc_ref[...] + sgb_ref[...].astype(F32) * y_attn
    o_ref[...] = x_ref[...] + _dot(merged.astype(BF16), wo_ref[0].astype(BF16))


def _mix_out(x, gc, sgb, attn, layer, w_ao, w_o):
    t = x.shape[0]
    tm = ROW_TILE
    row = lambda width: pl.BlockSpec((tm, width), lambda i: (i, 0))
    return pl.pallas_call(
        _mix_out_kernel,
        grid=(t // tm,),
        in_specs=[row(D_MODEL), row(D_MODEL), row(D_MODEL), row(ATTN_WIDTH),
                  _layer_spec((ATTN_WIDTH, D_MODEL), layer), _layer_spec((D_MODEL, D_MODEL), layer)],
        out_specs=row(D_MODEL),
        out_shape=jax.ShapeDtypeStruct((t, D_MODEL), F32),
        compiler_params=pltpu.CompilerParams(
            dimension_semantics=("parallel",), vmem_limit_bytes=VMEM_LIMIT_BYTES),
        name="mix_out",
    )(x, gc, sgb, attn, w_ao, w_o)


def _ffn_kernel(x_ref, g_ref, wi_ref, wo_ref, o_ref, acc_ref):
    x = x_ref[...]
    inv = lax.rsqrt(jnp.mean(x * x, axis=-1, keepdims=True) + EPS)
    h = ((x * inv) * g_ref[0]).astype(BF16)
    c0 = 0
    for n, width in enumerate(FFN_CHUNKS):
        gate = _dot(h, wi_ref[0, :, c0:c0 + width].astype(BF16))
        up = _dot(h, wi_ref[0, :, D_FF + c0:D_FF + c0 + width].astype(BF16))
        act = ((gate * _sigmoid(gate)) * up).astype(BF16)
        part = _dot(act, wo_ref[0, c0:c0 + width, :].astype(BF16))
        if n == 0:
            acc_ref[...] = part
        else:
            acc_ref[...] += part
        c0 += width
    o_ref[...] = x + acc_ref[...]


def _ffn(x, layer, g, w_i, w_o):
    t = x.shape[0]
    tm = ROW_TILE
    row = pl.BlockSpec((tm, D_MODEL), lambda i: (i, 0))
    return pl.pallas_call(
        _ffn_kernel,
        grid=(t // tm,),
        in_specs=[row, _layer_spec((1, D_MODEL), layer), _layer_spec((D_MODEL, 2 * D_FF), layer),
                  _layer_spec((D_FF, D_MODEL), layer)],
        out_specs=row,
        out_shape=jax.ShapeDtypeStruct((t, D_MODEL), F32),
        scratch_shapes=[pltpu.VMEM((tm, D_MODEL), F32)],
        compiler_params=pltpu.CompilerParams(
            dimension_semantics=("parallel",), vmem_limit_bytes=VMEM_LIMIT_BYTES),
        name="ffn",
    )(x, g, w_i, w_o)


def kernel(x, g_mix, w_in, b_forget, w_dw, b_dw, g_conv_ln, b_conv_ln, w_conv_out,
           g_q, g_k, w_attn_out, w_out, g_ffn, w_ffn_in, w_ffn_out):
    bsz, seq, d = x.shape
    depth = w_in.shape[0]
    f0 = 2 * CONV_CH + 3 * ATTN_WIDTH

    def spread(f):
        gap = jnp.zeros(f.shape[:-1] + (F_STRIDE - N_HEADS,), f.dtype)
        tail = jnp.zeros(f.shape[:-1] + (F_PAD - F_COPIES * F_STRIDE,), f.dtype)
        return jnp.concatenate([f, gap] * F_COPIES + [tail], axis=-1)

    w_all = jnp.concatenate(
        [w_in[:, :, :f0], w_in[:, :, f0 + N_HEADS:], spread(w_in[:, :, f0:f0 + N_HEADS])], axis=2).astype(BF16)
    bf_pad = spread(b_forget)[:, None, :]
    gq = (jnp.tile(g_q, (1, N_HEADS)) * (HEAD_DIM ** -0.5 * LOG2E))[:, None, :]
    gk = jnp.tile(g_k, (1, N_HEADS))[:, None, :]
    w_dw_pad = jnp.pad(w_dw, ((0, 0), (0, HALO - CONV_WIDTH), (0, 0)))
    vec = lambda p: p[:, None, :]

    for l in range(depth):
        a, qa, ka, v, sga, sgb = _in_proj(x, l, vec(g_mix), w_all, bf_pad, gq, gk)
        gc = _conv_branch(a, l, w_dw_pad, vec(b_dw), vec(g_conv_ln), vec(b_conv_ln), w_conv_out, sga)
        attn = _attention(qa, ka, v)
        x1 = _mix_out(x.reshape(bsz * seq, d), gc.reshape(bsz * seq, d), sgb.reshape(bsz * seq, d),
                      attn.reshape(bsz * seq, ATTN_WIDTH), l, w_attn_out, w_out)
        x = _ffn(x1, l, vec(g_ffn), w_ffn_in, w_ffn_out).reshape(bsz, seq, d)
    return x
```

```python
import functools

import jax
import jax.numpy as jnp
import numpy as np
from jax import lax
from jax.experimental import pallas as pl
from jax.experimental.pallas import tpu as pltpu

D_MODEL = 1024
CONV_CH = 512
CONV_WIDTH = 31
N_HEADS = 8
HEAD_DIM = 64
ATTN_WIDTH = N_HEADS * HEAD_DIM
D_FF = 2816
EPS = 1e-6
LOG2E = 1.4426950408889634

LANES = 128
SUBLANES = 8
HALO = 32
F_PAD = LANES
F_COPIES = 3
F_STRIDE = 16
ONES_LANE = N_HEADS
IN_COLS_PADDED = 2 * CONV_CH + 3 * ATTN_WIDTH + 2 * D_MODEL + F_PAD
VMEM_LIMIT_BYTES = 56 * 1024 * 1024

ROW_TILE = 512
CUMSUM_CHUNK = 256
Q_TILE = 512
KV_TILE = 512
FFN_CHUNKS = (512, 512, 512, 512, 512, 256)

BF16 = jnp.bfloat16
F32 = jnp.float32


def _dot(a, b):
    return jnp.dot(a, b, preferred_element_type=F32)


def _split3(x):
    hi = x.astype(BF16)
    r = x - hi.astype(F32)
    mid = r.astype(BF16)
    lo = (r - mid.astype(F32)).astype(BF16)
    return hi, mid, lo


def _sigmoid(x):
    return 1.0 / (1.0 + jnp.exp(-x))


def _const_spec(shape):
    nd = len(shape)
    return pl.BlockSpec(shape, lambda *_: (0,) * nd, pipeline_mode=pl.Buffered(1))


def _layer_spec(shape, layer):
    nd = len(shape)
    return pl.BlockSpec((1,) + tuple(shape), lambda *_: (layer,) + (0,) * nd, pipeline_mode=pl.Buffered(1))


def _head_rmsnorm(p, gain):
    rows = p.shape[0]
    low = lax.broadcasted_iota(jnp.int32, (rows, LANES), 1) < HEAD_DIM
    out = []
    for g in range(p.shape[1] // LANES):
        blk = p[:, g * LANES:(g + 1) * LANES]
        sq = blk * blk
        s_low = jnp.sum(jnp.where(low, sq, 0.0), axis=1, keepdims=True)
        s_high = jnp.sum(jnp.where(low, 0.0, sq), axis=1, keepdims=True)
        ms = jnp.where(low, s_low, s_high) * (1.0 / HEAD_DIM)
        out.append(blk * lax.rsqrt(ms + EPS))
    return jnp.concatenate(out, axis=1) * gain


def _in_proj_kernel(x_ref, g_ref, w_ref, bf_ref, gq_ref, gk_ref, tri_ref, place_ref,
                    a_ref, qa_ref, ka_ref, v_ref, sga_ref, sgb_ref, carry_ref):
    tm = x_ref.shape[1]

    @pl.when(pl.program_id(1) == 0)
    def _():
        carry_ref[...] = jnp.zeros_like(carry_ref)

    x = x_ref[0]
    inv = lax.rsqrt(jnp.mean(x * x, axis=-1, keepdims=True) + EPS)
    h = ((x * inv) * g_ref[0]).astype(BF16)

    c0 = 0
    pa = _dot(h, w_ref[0, :, c0:c0 + 2 * CONV_CH])
    a_ref[0] = pa[:, :CONV_CH] * _sigmoid(pa[:, CONV_CH:])
    c0 += 2 * CONV_CH
    qn = _head_rmsnorm(_dot(h, w_ref[0, :, c0:c0 + ATTN_WIDTH]), gq_ref[0])
    c0 += ATTN_WIDTH
    kn = _head_rmsnorm(_dot(h, w_ref[0, :, c0:c0 + ATTN_WIDTH]), gk_ref[0])
    c0 += ATTN_WIDTH
    v_ref[0] = _dot(h, w_ref[0, :, c0:c0 + ATTN_WIDTH]).astype(BF16)
    c0 += ATTN_WIDTH
    sga_ref[0] = _sigmoid(_dot(h, w_ref[0, :, c0:c0 + D_MODEL])).astype(BF16)
    c0 += D_MODEL
    sgb_ref[0] = _sigmoid(_dot(h, w_ref[0, :, c0:c0 + D_MODEL])).astype(BF16)
    c0 += D_MODEL

    z = _dot(h, w_ref[0, :, c0:c0 + F_PAD]) + bf_ref[0]
    lane = lax.broadcasted_iota(jnp.int32, (tm, F_PAD), 1)
    valid = (lane < F_COPIES * F_STRIDE) & ((lane & N_HEADS) == 0)
    logf = jnp.where(valid, jnp.minimum(z, 0.0) - jnp.log1p(jnp.exp(-jnp.abs(z))), 0.0)
    hi, mid, lo = _split3(logf)
    parts = jnp.concatenate([hi, mid, lo], axis=1)
    carry = carry_ref[0:1, :]
    chunks = []
    for ch in range(tm // CUMSUM_CHUNK):
        y = _dot(tri_ref[...], parts[ch * CUMSUM_CHUNK:(ch + 1) * CUMSUM_CHUNK])
        c = (y[:, :F_PAD] + y[:, F_PAD:2 * F_PAD]) + y[:, 2 * F_PAD:] + carry
        carry = c[CUMSUM_CHUNK - 1:CUMSUM_CHUNK, :]
        chunks.append(c)
    cum = jnp.concatenate(chunks, axis=0)
    carry_ref[...] = jnp.broadcast_to(carry, carry_ref.shape)

    cum1 = jnp.where(lane == ONES_LANE, 1.0, cum * LOG2E)
    hi, mid, lo = _split3(cum1)
    packed = jnp.where(lane < F_STRIDE, hi.astype(F32),
                       jnp.where(lane < 2 * F_STRIDE, mid.astype(F32), lo.astype(F32))).astype(BF16)
    aug = _dot(packed, place_ref[...])
    low = lax.broadcasted_iota(jnp.int32, (tm, LANES), 1) < HEAD_DIM
    for pr in range(N_HEADS // 2):
        pair = slice(pr * LANES, (pr + 1) * LANES)
        kpair = slice(ATTN_WIDTH + pr * LANES, ATTN_WIDTH + (pr + 1) * LANES)
        qa_ref[0, 2 * pr] = jnp.where(low, qn[:, pair], aug[:, pair]).astype(BF16)
        qa_ref[0, 2 * pr + 1] = jnp.where(low, aug[:, pair], qn[:, pair]).astype(BF16)
        ka_ref[0, 2 * pr] = jnp.where(low, kn[:, pair], aug[:, kpair]).astype(BF16)
        ka_ref[0, 2 * pr + 1] = jnp.where(low, aug[:, kpair], kn[:, pair]).astype(BF16)


def _placement_matrix():
    p = np.zeros((F_PAD, 2 * ATTN_WIDTH), np.float32)
    for hd in range(N_HEADS):
        base = (hd // 2) * LANES + (HEAD_DIM if hd % 2 == 0 else 0)
        kbase = ATTN_WIDTH + base
        for part in range(F_COPIES):
            p[part * F_STRIDE + hd, base + part] = 1.0
            p[ONES_LANE, base + 3 + part] = 1.0
            p[ONES_LANE, kbase + part] = 1.0
            p[part * F_STRIDE + hd, kbase + 3 + part] = -1.0
    return p


def _in_proj(x, layer, g, w_all, b_forget_pad, gq, gk):
    bsz, seq, _ = x.shape
    tm = ROW_TILE
    tri = jnp.asarray(np.tril(np.ones((CUMSUM_CHUNK, CUMSUM_CHUNK), np.float32)), BF16)
    place = jnp.asarray(_placement_matrix(), BF16)
    row = lambda width: pl.BlockSpec((1, tm, width), lambda b, i: (b, i, 0))
    heads = pl.BlockSpec((1, N_HEADS, tm, LANES), lambda b, i: (b, 0, i, 0))
    return pl.pallas_call(
        _in_proj_kernel,
        grid=(bsz, seq // tm),
        in_specs=[row(D_MODEL), _layer_spec((1, D_MODEL), layer), _layer_spec((D_MODEL, IN_COLS_PADDED), layer),
                  _layer_spec((1, F_PAD), layer), _layer_spec((1, ATTN_WIDTH), layer),
                  _layer_spec((1, ATTN_WIDTH), layer), _const_spec(tri.shape), _const_spec(place.shape)],
        out_specs=[row(CONV_CH), heads, heads, row(ATTN_WIDTH), row(D_MODEL), row(D_MODEL)],
        out_shape=[jax.ShapeDtypeStruct((bsz, seq, CONV_CH), F32),
                   jax.ShapeDtypeStruct((bsz, N_HEADS, seq, LANES), BF16),
                   jax.ShapeDtypeStruct((bsz, N_HEADS, seq, LANES), BF16),
                   jax.ShapeDtypeStruct((bsz, seq, ATTN_WIDTH), BF16),
                   jax.ShapeDtypeStruct((bsz, seq, D_MODEL), BF16),
                   jax.ShapeDtypeStruct((bsz, seq, D_MODEL), BF16)],
        scratch_shapes=[pltpu.VMEM((SUBLANES, F_PAD), F32)],
        compiler_params=pltpu.CompilerParams(
            dimension_semantics=("arbitrary", "arbitrary"), vmem_limit_bytes=VMEM_LIMIT_BYTES),
        name="in_proj",
    )(x, g, w_all, b_forget_pad, gq, gk, tri, place)


def _attn_kernel(qa_ref, ka_ref, v_ref, o_ref, vext_ref, m_ref, acc_ref, sa_ref, sb_ref):
    tq = qa_ref.shape[2]
    tk = KV_TILE
    qi = pl.program_id(2)

    @pl.when(qi == 0)
    def _():
        vext_ref[:, :LANES] = v_ref[0]
        vext_ref[:, LANES:] = jnp.ones((vext_ref.shape[0], LANES), BF16)

    m_ref[...] = jnp.full(m_ref.shape, -jnp.inf, F32)
    acc_ref[...] = jnp.zeros_like(acc_ref)

    def scores(j, s_ref):
        start = pl.multiple_of(j * tk, tk)
        for e in range(2):
            k = ka_ref[0, e, pl.ds(start, tk), :]
            s_ref[e] = lax.dot_general(qa_ref[0, e], k, (((1,), (1,)), ((), ())), preferred_element_type=F32)

    def update(j, s_ref, masked):
        start = pl.multiple_of(j * tk, tk)
        vext = vext_ref[pl.ds(start, tk), :]
        for e in range(2):
            s = s_ref[e]
            if masked:
                row = lax.broadcasted_iota(jnp.int32, (tq, tk), 0)
                col = lax.broadcasted_iota(jnp.int32, (tq, tk), 1)
                s = jnp.where(col <= row, s, -jnp.inf)
            m_prev = m_ref[e]
            m_new = jnp.maximum(m_prev, jnp.max(s, axis=1, keepdims=True))
            alpha = jnp.exp2(m_prev - m_new)
            p = jnp.exp2(s - jnp.concatenate([m_new] * (tk // LANES), axis=1))
            pv = _dot(p.astype(BF16), vext)
            acc_ref[e] = jnp.concatenate([alpha, alpha], axis=1) * acc_ref[e] + pv
            m_ref[e] = m_new

    scores(0, sa_ref)

    def pair(t, carry):
        j = 2 * t
        scores(j + 1, sb_ref)
        update(j, sa_ref, masked=False)
        scores(j + 2, sa_ref)
        update(j + 1, sb_ref, masked=False)
        return carry

    lax.fori_loop(0, qi // 2, pair, 0)

    @pl.when(qi % 2 == 1)
    def _():
        scores(qi, sb_ref)
        update(qi - 1, sa_ref, masked=False)
        update(qi, sb_ref, masked=True)

    @pl.when(qi % 2 == 0)
    def _():
        update(qi, sa_ref, masked=True)

    o0 = acc_ref[0, :, :LANES] / acc_ref[0, :, LANES:]
    o1 = acc_ref[1, :, :LANES] / acc_ref[1, :, LANES:]
    lane = lax.broadcasted_iota(jnp.int32, (tq, LANES), 1)
    o_ref[0] = jnp.where(lane < HEAD_DIM, o0, o1).astype(BF16)


def _attention(qa, ka, v):
    bsz, _, seq, _ = qa.shape
    assert Q_TILE == KV_TILE
    tq = Q_TILE
    return pl.pallas_call(
        _attn_kernel,
        grid=(bsz, N_HEADS // 2, seq // tq),
        in_specs=[pl.BlockSpec((1, 2, tq, LANES), lambda b, p, i: (b, p, i, 0)),
                  pl.BlockSpec((1, 2, seq, LANES), lambda b, p, i: (b, p, 0, 0)),
                  pl.BlockSpec((1, seq, LANES), lambda b, p, i: (b, 0, p))],
        out_specs=pl.BlockSpec((1, tq, LANES), lambda b, p, i: (b, i, p)),
        out_shape=jax.ShapeDtypeStruct((bsz, seq, ATTN_WIDTH), BF16),
        scratch_shapes=[pltpu.VMEM((seq, 2 * LANES), BF16),
                        pltpu.VMEM((2, tq, LANES), F32),
                        pltpu.VMEM((2, tq, 2 * LANES), F32),
                        pltpu.VMEM((2, tq, KV_TILE), F32),
                        pltpu.VMEM((2, tq, KV_TILE), F32)],
        compiler_params=pltpu.CompilerParams(
            dimension_semantics=("arbitrary", "arbitrary", "arbitrary"), vmem_limit_bytes=VMEM_LIMIT_BYTES),
        name="fox_attention",
    )(qa, ka, v)


def _post_kernel(x_ref, a_ref, halo_ref, sga_ref, sgb_ref, attn_ref,
                 wdw_ref, bdw_ref, gln_ref, bln_ref, wco_ref, wao_ref, wo_ref, gffn_ref, wi_ref, wfo_ref,
                 o_ref, act_ref, win_ref, conv_ref, h_ref, *, tiles_per_seq):
    ts = a_ref.shape[0]
    step = pl.program_id(0)

    @pl.when(step == 0)
    def _():
        act_ref[...] = jnp.zeros_like(act_ref)

    seq_start = (step % tiles_per_seq) == 0
    lead = HALO - (CONV_WIDTH - 1)
    shifted_rows = ts + HALO - SUBLANES
    row_groups = ts // SUBLANES
    n_units = (CONV_CH // LANES) * row_groups

    def zero_after(values):
        bits = [lax.bitcast_convert_type(v, jnp.uint32) for v in values]
        while len(bits) > 1:
            bits = [bits[i] | bits[i + 1] for i in range(0, len(bits) - 1, 2)] + bits[len(bits) & ~1:]
        z = lax.shift_right_logical(lax.shift_right_logical(bits[0], jnp.uint32(16)), jnp.uint32(16))
        return lax.bitcast_convert_type(z, F32)

    def conv_units(first, last):
        accs = []
        for unit in range(first, min(last, n_units)):
            g, r = divmod(unit, row_groups)
            cols = slice(g * LANES, (g + 1) * LANES)
            if r == 0:
                win_ref[g, 0, 0:HALO, :] = jnp.where(seq_start, 0.0, halo_ref[:, cols])
                win_ref[g, 0, HALO:, :] = a_ref[:, cols]
                for k in range(1, SUBLANES):
                    win_ref[g, k, 0:shifted_rows, :] = win_ref[g, 0, k:k + shifted_rows, :]
            acc = jnp.broadcast_to(bdw_ref[0][:, cols], (SUBLANES, LANES))
            for j in range(CONV_WIDTH):
                off = lead + j
                row0 = r * SUBLANES + (off // SUBLANES) * SUBLANES
                acc = acc + wdw_ref[0, j:j + 1, cols] * win_ref[g, off % SUBLANES, row0:row0 + SUBLANES, :]
            conv_ref[r * SUBLANES:(r + 1) * SUBLANES, cols] = acc
            accs.append(acc)
        return accs

    y_conv = _dot(act_ref[...], wco_ref[0])
    y_attn = _dot(attn_ref[...], wao_ref[0])
    merged = sga_ref[...].astype(F32) * y_conv + sgb_ref[...].astype(F32) * y_attn
    x1 = x_ref[...] + _dot(merged.astype(BF16), wo_ref[0])
    inv = lax.rsqrt(jnp.mean(x1 * x1, axis=-1, keepdims=True) + EPS)
    h_ref[...] = ((x1 * inv) * gffn_ref[0]).astype(BF16)
    o_ref[...] = x1
    per_chunk = -(-n_units // (len(FFN_CHUNKS) - 1))
    c0 = 0
    for n, width in enumerate(FFN_CHUNKS):
        gate = _dot(h_ref[...], wi_ref[0, :, c0:c0 + width])
        up = _dot(h_ref[...], wi_ref[0, :, D_FF + c0:D_FF + c0 + width])
        hidden = ((gate * _sigmoid(gate)) * up).astype(BF16)
        if n < len(FFN_CHUNKS) - 1:
            tie = conv_units(n * per_chunk, (n + 1) * per_chunk)
        else:
            c = conv_ref[...]
            mu = jnp.mean(c, axis=-1, keepdims=True)
            d = c - mu
            var = jnp.mean(d * d, axis=-1, keepdims=True)
            y = (d * lax.rsqrt(var + EPS)) * gln_ref[0] + bln_ref[0]
            y = y * _sigmoid(y)
            act_ref[...] = y.astype(BF16)
            tie = [y[r * SUBLANES:(r + 1) * SUBLANES, g * LANES:(g + 1) * LANES]
                   for r in range(row_groups) for g in range(CONV_CH // LANES)]
        o_ref[0:SUBLANES, 0:LANES] += zero_after(tie)
        o_ref[...] += _dot(hidden, wfo_ref[0, c0:c0 + width, :])
        c0 += width


def _post(x, a, sga, sgb, attn, layer, seq, w_dw, b_dw, g_ln, b_ln, w_co, w_ao, w_o, g_ffn, w_fi, w_fo):
    t = x.shape[0]
    tm = ROW_TILE
    n_tiles = t // tm
    per_tile = tm // HALO
    prev = lambda width: pl.BlockSpec((tm, width), lambda s: (jnp.maximum(s - 1, 0), 0))
    cur = lambda s: jnp.minimum(s, n_tiles - 1)
    return pl.pallas_call(
        functools.partial(_post_kernel, tiles_per_seq=seq // tm),
        grid=(n_tiles + 1,),
        in_specs=[prev(D_MODEL),
                  pl.BlockSpec((tm, CONV_CH), lambda s: (cur(s), 0)),
                  pl.BlockSpec((HALO, CONV_CH), lambda s: (jnp.maximum(cur(s) * per_tile - 1, 0), 0)),
                  prev(D_MODEL), prev(D_MODEL), prev(ATTN_WIDTH),
                  _layer_spec((HALO, CONV_CH), layer), _layer_spec((1, CONV_CH), layer),
                  _layer_spec((1, CONV_CH), layer), _layer_spec((1, CONV_CH), layer),
                  _layer_spec((CONV_CH, D_MODEL), layer), _layer_spec((ATTN_WIDTH, D_MODEL), layer),
                  _layer_spec((D_MODEL, D_MODEL), layer), _layer_spec((1, D_MODEL), layer),
                  _layer_spec((D_MODEL, 2 * D_FF), layer), _layer_spec((D_FF, D_MODEL), layer)],
        out_specs=prev(D_MODEL),
        out_shape=jax.ShapeDtypeStruct((t, D_MODEL), F32),
        scratch_shapes=[pltpu.VMEM((tm, CONV_CH), BF16),
                        pltpu.VMEM((CONV_CH // LANES, SUBLANES, HALO + tm, LANES), F32),
                        pltpu.VMEM((tm, CONV_CH), F32),
                        pltpu.VMEM((tm, D_MODEL), BF16)],
        compiler_params=pltpu.CompilerParams(
            dimension_semantics=("arbitrary",), vmem_limit_bytes=VMEM_LIMIT_BYTES),
        name="post",
    )(x, a, a, sga, sgb, attn, w_dw, b_dw, g_ln, b_ln, w_co, w_ao, w_o, g_ffn, w_fi, w_fo)


def kernel(x, g_mix, w_in, b_forget, w_dw, b_dw, g_conv_ln, b_conv_ln, w_conv_out,
           g_q, g_k, w_attn_out, w_out, g_ffn, w_ffn_in, w_ffn_out):
    bsz, seq, d = x.shape
    depth = w_in.shape[0]
    f0 = 2 * CONV_CH + 3 * ATTN_WIDTH

    def spread(f):
        gap = jnp.zeros(f.shape[:-1] + (F_STRIDE - N_HEADS,), f.dtype)
        tail = jnp.zeros(f.shape[:-1] + (F_PAD - F_COPIES * F_STRIDE,), f.dtype)
        return jnp.concatenate([f, gap] * F_COPIES + [tail], axis=-1)

    w_all = jnp.concatenate(
        [w_in[:, :, :f0], w_in[:, :, f0 + N_HEADS:], spread(w_in[:, :, f0:f0 + N_HEADS])], axis=2).astype(BF16)
    bf_pad = spread(b_forget)[:, None, :]
    gq = (jnp.tile(g_q, (1, N_HEADS)) * (HEAD_DIM ** -0.5 * LOG2E))[:, None, :]
    gk = jnp.tile(g_k, (1, N_HEADS))[:, None, :]
    w_dw_pad = jnp.pad(w_dw, ((0, 0), (0, HALO - CONV_WIDTH), (0, 0)))
    vec = lambda p: p[:, None, :]
    w_co, w_ao, w_o = w_conv_out.astype(BF16), w_attn_out.astype(BF16), w_out.astype(BF16)
    w_fi, w_fo = w_ffn_in.astype(BF16), w_ffn_out.astype(BF16)

    t = bsz * seq
    for l in range(depth):
        a, qa, ka, v, sga, sgb = _in_proj(x, l, vec(g_mix), w_all, bf_pad, gq, gk)
        attn = _attention(qa, ka, v)
        x = _post(x.reshape(t, d), a.reshape(t, CONV_CH), sga.reshape(t, d), sgb.reshape(t, d),
                  attn.reshape(t, ATTN_WIDTH), l, seq, w_dw_pad, vec(b_dw), vec(g_conv_ln), vec(b_conv_ln),
                  w_co, w_ao, w_o, vec(g_ffn), w_fi, w_fo).reshape(bsz, seq, d)
    return x
```

```python
import functools

import jax
import jax.numpy as jnp
import numpy as np
from jax import lax
from jax.experimental import pallas as pl
from jax.experimental.pallas import tpu as pltpu

D_MODEL = 1024
CONV_CH = 512
CONV_WIDTH = 31
N_HEADS = 8
HEAD_DIM = 64
ATTN_WIDTH = N_HEADS * HEAD_DIM
D_FF = 2816
EPS = 1e-6
LOG2E = 1.4426950408889634

LANES = 128
SUBLANES = 8
HALO = 32
F_PAD = LANES
F_COPIES = 3
F_STRIDE = 16
ONES_LANE = N_HEADS
IN_COLS_PADDED = 2 * CONV_CH + 3 * ATTN_WIDTH + 2 * D_MODEL + F_PAD
VMEM_LIMIT_BYTES = 56 * 1024 * 1024

ROW_TILE = 512
CUMSUM_CHUNK = 256
Q_TILE = 1024
KV_TILE = 512
FFN_CHUNKS = (512, 512, 512, 512, 512, 256)

BF16 = jnp.bfloat16
F32 = jnp.float32


def _dot(a, b):
    return jnp.dot(a, b, preferred_element_type=F32)


def _split3(x):
    hi = x.astype(BF16)
    r = x - hi.astype(F32)
    mid = r.astype(BF16)
    lo = (r - mid.astype(F32)).astype(BF16)
    return hi, mid, lo


def _sigmoid(x):
    return 1.0 / (1.0 + jnp.exp(-x))


def _const_spec(shape):
    nd = len(shape)
    return pl.BlockSpec(shape, lambda *_: (0,) * nd, pipeline_mode=pl.Buffered(1))


def _layer_spec(shape, layer):
    nd = len(shape)
    return pl.BlockSpec((1,) + tuple(shape), lambda *_: (layer,) + (0,) * nd, pipeline_mode=pl.Buffered(1))


def _head_rmsnorm(p, gain):
    rows = p.shape[0]
    low = lax.broadcasted_iota(jnp.int32, (rows, LANES), 1) < HEAD_DIM
    out = []
    for g in range(p.shape[1] // LANES):
        blk = p[:, g * LANES:(g + 1) * LANES]
        sq = blk * blk
        s_low = jnp.sum(jnp.where(low, sq, 0.0), axis=1, keepdims=True)
        s_high = jnp.sum(jnp.where(low, 0.0, sq), axis=1, keepdims=True)
        ms = jnp.where(low, s_low, s_high) * (1.0 / HEAD_DIM)
        out.append(blk * lax.rsqrt(ms + EPS))
    return jnp.concatenate(out, axis=1) * gain


def _in_proj_kernel(x_ref, g_ref, w_ref, bf_ref, gq_ref, gk_ref, tri_ref, place_ref,
                    a_ref, qa_ref, ka_ref, v_ref, sga_ref, sgb_ref, carry_ref):
    tm = x_ref.shape[1]

    @pl.when(pl.program_id(1) == 0)
    def _():
        carry_ref[...] = jnp.zeros_like(carry_ref)

    x = x_ref[0]
    inv = lax.rsqrt(jnp.mean(x * x, axis=-1, keepdims=True) + EPS)
    h = ((x * inv) * g_ref[0]).astype(BF16)

    c0 = 0
    pa = _dot(h, w_ref[0, :, c0:c0 + 2 * CONV_CH])
    a_ref[0] = pa[:, :CONV_CH] * _sigmoid(pa[:, CONV_CH:])
    c0 += 2 * CONV_CH
    qn = _head_rmsnorm(_dot(h, w_ref[0, :, c0:c0 + ATTN_WIDTH]), gq_ref[0])
    c0 += ATTN_WIDTH
    kn = _head_rmsnorm(_dot(h, w_ref[0, :, c0:c0 + ATTN_WIDTH]), gk_ref[0])
    c0 += ATTN_WIDTH
    v_ref[0] = _dot(h, w_ref[0, :, c0:c0 + ATTN_WIDTH]).astype(BF16)
    c0 += ATTN_WIDTH
    sga_ref[0] = _sigmoid(_dot(h, w_ref[0, :, c0:c0 + D_MODEL])).astype(BF16)
    c0 += D_MODEL
    sgb_ref[0] = _sigmoid(_dot(h, w_ref[0, :, c0:c0 + D_MODEL])).astype(BF16)
    c0 += D_MODEL

    z = _dot(h, w_ref[0, :, c0:c0 + F_PAD]) + bf_ref[0]
    lane = lax.broadcasted_iota(jnp.int32, (tm, F_PAD), 1)
    valid = (lane < F_COPIES * F_STRIDE) & ((lane & N_HEADS) == 0)
    logf = jnp.where(valid, jnp.minimum(z, 0.0) - jnp.log1p(jnp.exp(-jnp.abs(z))), 0.0)
    hi, mid, lo = _split3(logf)
    parts = jnp.concatenate([hi, mid, lo], axis=1)
    carry = carry_ref[0:1, :]
    chunks = []
    for ch in range(tm // CUMSUM_CHUNK):
        y = _dot(tri_ref[...], parts[ch * CUMSUM_CHUNK:(ch + 1) * CUMSUM_CHUNK])
        c = (y[:, :F_PAD] + y[:, F_PAD:2 * F_PAD]) + y[:, 2 * F_PAD:] + carry
        carry = c[CUMSUM_CHUNK - 1:CUMSUM_CHUNK, :]
        chunks.append(c)
    cum = jnp.concatenate(chunks, axis=0)
    carry_ref[...] = jnp.broadcast_to(carry, carry_ref.shape)

    cum1 = jnp.where(lane == ONES_LANE, 1.0, cum * LOG2E)
    hi, mid, lo = _split3(cum1)
    packed = jnp.where(lane < F_STRIDE, hi.astype(F32),
                       jnp.where(lane < 2 * F_STRIDE, mid.astype(F32), lo.astype(F32))).astype(BF16)
    aug = _dot(packed, place_ref[...])
    low = lax.broadcasted_iota(jnp.int32, (tm, LANES), 1) < HEAD_DIM
    for pr in range(N_HEADS // 2):
        pair = slice(pr * LANES, (pr + 1) * LANES)
        kpair = slice(ATTN_WIDTH + pr * LANES, ATTN_WIDTH + (pr + 1) * LANES)
        qa_ref[0, 2 * pr] = jnp.where(low, qn[:, pair], aug[:, pair]).astype(BF16)
        qa_ref[0, 2 * pr + 1] = jnp.where(low, aug[:, pair], qn[:, pair]).astype(BF16)
        ka_ref[0, 2 * pr] = jnp.where(low, kn[:, pair], aug[:, kpair]).astype(BF16)
        ka_ref[0, 2 * pr + 1] = jnp.where(low, aug[:, kpair], kn[:, pair]).astype(BF16)


def _placement_matrix():
    p = np.zeros((F_PAD, 2 * ATTN_WIDTH), np.float32)
    for hd in range(N_HEADS):
        base = (hd // 2) * LANES + (HEAD_DIM if hd % 2 == 0 else 0)
        kbase = ATTN_WIDTH + base
        for part in range(F_COPIES):
            p[part * F_STRIDE + hd, base + part] = 1.0
            p[ONES_LANE, base + 3 + part] = 1.0
            p[ONES_LANE, kbase + part] = 1.0
            p[part * F_STRIDE + hd, kbase + 3 + part] = -1.0
    return p


def _in_proj(x, layer, g, w_all, b_forget_pad, gq, gk):
    bsz, seq, _ = x.shape
    tm = ROW_TILE
    tri = jnp.asarray(np.tril(np.ones((CUMSUM_CHUNK, CUMSUM_CHUNK), np.float32)), BF16)
    place = jnp.asarray(_placement_matrix(), BF16)
    row = lambda width: pl.BlockSpec((1, tm, width), lambda b, i: (b, i, 0))
    heads = pl.BlockSpec((1, N_HEADS, tm, LANES), lambda b, i: (b, 0, i, 0))
    return pl.pallas_call(
        _in_proj_kernel,
        grid=(bsz, seq // tm),
        in_specs=[row(D_MODEL), _layer_spec((1, D_MODEL), layer), _layer_spec((D_MODEL, IN_COLS_PADDED), layer),
                  _layer_spec((1, F_PAD), layer), _layer_spec((1, ATTN_WIDTH), layer),
                  _layer_spec((1, ATTN_WIDTH), layer), _const_spec(tri.shape), _const_spec(place.shape)],
        out_specs=[row(CONV_CH), heads, heads, row(ATTN_WIDTH), row(D_MODEL), row(D_MODEL)],
        out_shape=[jax.ShapeDtypeStruct((bsz, seq, CONV_CH), F32),
                   jax.ShapeDtypeStruct((bsz, N_HEADS, seq, LANES), BF16),
                   jax.ShapeDtypeStruct((bsz, N_HEADS, seq, LANES), BF16),
                   jax.ShapeDtypeStruct((bsz, seq, ATTN_WIDTH), BF16),
                   jax.ShapeDtypeStruct((bsz, seq, D_MODEL), BF16),
                   jax.ShapeDtypeStruct((bsz, seq, D_MODEL), BF16)],
        scratch_shapes=[pltpu.VMEM((SUBLANES, F_PAD), F32)],
        compiler_params=pltpu.CompilerParams(
            dimension_semantics=("arbitrary", "arbitrary"), vmem_limit_bytes=VMEM_LIMIT_BYTES),
        name="in_proj",
    )(x, g, w_all, b_forget_pad, gq, gk, tri, place)


def _attn_kernel(qa_ref, ka_ref, v_ref, o_ref, vext_ref, m_ref, acc_ref, sa_ref, sb_ref):
    tq = qa_ref.shape[2]
    tk = KV_TILE
    qi = pl.program_id(2)
    top, bottom, whole = (0, tk), (tk, tq), (0, tq)

    @pl.when(qi == 0)
    def _():
        vext_ref[:, :LANES] = v_ref[0]
        vext_ref[:, LANES:] = jnp.ones((vext_ref.shape[0], LANES), BF16)

    m_ref[...] = jnp.full(m_ref.shape, -jnp.inf, F32)
    acc_ref[...] = jnp.zeros_like(acc_ref)

    def scores(j, s_ref, rows):
        start = pl.multiple_of(j * tk, tk)
        for e in range(2):
            k = ka_ref[0, e, pl.ds(start, tk), :]
            for r0 in range(rows[0], rows[1], tk):
                r = slice(r0, r0 + tk)
                s_ref[e, r] = lax.dot_general(qa_ref[0, e, r], k, (((1,), (1,)), ((), ())),
                                              preferred_element_type=F32)

    def update(j, s_ref, rows, causal):
        start = pl.multiple_of(j * tk, tk)
        vext = vext_ref[pl.ds(start, tk), :]
        for e in range(2):
            for r0 in range(rows[0], rows[1], tk):
                r = slice(r0, r0 + tk)
                s = s_ref[e, r]
                if causal:
                    row = lax.broadcasted_iota(jnp.int32, (tk, tk), 0)
                    col = lax.broadcasted_iota(jnp.int32, (tk, tk), 1)
                    s = jnp.where(col <= row, s, -jnp.inf)
                m_prev = m_ref[e, r]
                m_new = jnp.maximum(m_prev, jnp.max(s, axis=1, keepdims=True))
                alpha = jnp.exp2(m_prev - m_new)
                p = jnp.exp2(s - jnp.concatenate([m_new] * (tk // LANES), axis=1))
                acc_ref[e, r] = jnp.concatenate([alpha, alpha], axis=1) * acc_ref[e, r] + _dot(p.astype(BF16), vext)
                m_ref[e, r] = m_new

    scores(0, sa_ref, whole)

    def pair(t, carry):
        j = 2 * t
        scores(j + 1, sb_ref, whole)
        update(j, sa_ref, whole, causal=False)
        scores(j + 2, sa_ref, whole)
        update(j + 1, sb_ref, whole, causal=False)
        return carry

    lax.fori_loop(0, qi, pair, 0)

    scores(2 * qi + 1, sb_ref, bottom)
    update(2 * qi, sa_ref, top, causal=True)
    update(2 * qi, sa_ref, bottom, causal=False)
    update(2 * qi + 1, sb_ref, bottom, causal=True)

    o0 = acc_ref[0, :, :LANES] / acc_ref[0, :, LANES:]
    o1 = acc_ref[1, :, :LANES] / acc_ref[1, :, LANES:]
    lane = lax.broadcasted_iota(jnp.int32, (tq, LANES), 1)
    o_ref[0] = jnp.where(lane < HEAD_DIM, o0, o1).astype(BF16)


def _attention(qa, ka, v):
    bsz, _, seq, _ = qa.shape
    assert Q_TILE == 2 * KV_TILE
    tq = Q_TILE
    return pl.pallas_call(
        _attn_kernel,
        grid=(bsz, N_HEADS // 2, seq // tq),
        in_specs=[pl.BlockSpec((1, 2, tq, LANES), lambda b, p, i: (b, p, i, 0)),
                  pl.BlockSpec((1, 2, seq, LANES), lambda b, p, i: (b, p, 0, 0)),
                  pl.BlockSpec((1, seq, LANES), lambda b, p, i: (b, 0, p))],
        out_specs=pl.BlockSpec((1, tq, LANES), lambda b, p, i: (b, i, p)),
        out_shape=jax.ShapeDtypeStruct((bsz, seq, ATTN_WIDTH), BF16),
        scratch_shapes=[pltpu.VMEM((seq, 2 * LANES), BF16),
                        pltpu.VMEM((2, tq, LANES), F32),
                        pltpu.VMEM((2, tq, 2 * LANES), F32),
                        pltpu.VMEM((2, tq, KV_TILE), F32),
                        pltpu.VMEM((2, tq, KV_TILE), F32)],
        compiler_params=pltpu.CompilerParams(
            dimension_semantics=("arbitrary", "arbitrary", "arbitrary"), vmem_limit_bytes=VMEM_LIMIT_BYTES),
        name="fox_attention",
    )(qa, ka, v)


def _post_kernel(x_ref, a_ref, halo_ref, sga_ref, sgb_ref, attn_ref,
                 wdw_ref, bdw_ref, gln_ref, bln_ref, wco_ref, wao_ref, wo_ref, gffn_ref, wi_ref, wfo_ref,
                 o_ref, act_ref, win_ref, conv_ref, h_ref, *, tiles_per_seq):
    ts = a_ref.shape[0]
    step = pl.program_id(0)

    @pl.when(step == 0)
    def _():
        act_ref[...] = jnp.zeros_like(act_ref)

    seq_start = (step % tiles_per_seq) == 0
    lead = HALO - (CONV_WIDTH - 1)
    shifted_rows = ts + HALO - SUBLANES
    row_groups = ts // SUBLANES
    n_units = (CONV_CH // LANES) * row_groups

    def zero_after(values):
        bits = [lax.bitcast_convert_type(v, jnp.uint32) for v in values]
        while len(bits) > 1:
            bits = [bits[i] | bits[i + 1] for i in range(0, len(bits) - 1, 2)] + bits[len(bits) & ~1:]
        z = lax.shift_right_logical(lax.shift_right_logical(bits[0], jnp.uint32(16)), jnp.uint32(16))
        return lax.bitcast_convert_type(z, F32)

    def conv_units(first, last):
        accs = []
        for unit in range(first, min(last, n_units)):
            g, r = divmod(unit, row_groups)
            cols = slice(g * LANES, (g + 1) * LANES)
            if r == 0:
                win_ref[g, 0, 0:HALO, :] = jnp.where(seq_start, 0.0, halo_ref[:, cols])
                win_ref[g, 0, HALO:, :] = a_ref[:, cols]
                for k in range(1, SUBLANES):
                    win_ref[g, k, 0:shifted_rows, :] = win_ref[g, 0, k:k + shifted_rows, :]
            acc = jnp.broadcast_to(bdw_ref[0][:, cols], (SUBLANES, LANES))
            for j in range(CONV_WIDTH):
                off = lead + j
                row0 = r * SUBLANES + (off // SUBLANES) * SUBLANES
                acc = acc + wdw_ref[0, j:j + 1, cols] * win_ref[g, off % SUBLANES, row0:row0 + SUBLANES, :]
            conv_ref[r * SUBLANES:(r + 1) * SUBLANES, cols] = acc
            accs.append(acc)
        return accs

    y_conv = _dot(act_ref[...], wco_ref[0])
    y_attn = _dot(attn_ref[...], wao_ref[0])
    merged = sga_ref[...].astype(F32) * y_conv + sgb_ref[...].astype(F32) * y_attn
    x1 = x_ref[...] + _dot(merged.astype(BF16), wo_ref[0])
    inv = lax.rsqrt(jnp.mean(x1 * x1, axis=-1, keepdims=True) + EPS)
    h_ref[...] = ((x1 * inv) * gffn_ref[0]).astype(BF16)
    o_ref[...] = x1
    per_chunk = -(-n_units // (len(FFN_CHUNKS) - 1))
    c0 = 0
    for n, width in enumerate(FFN_CHUNKS):
        gate = _dot(h_ref[...], wi_ref[0, :, c0:c0 + width])
        up = _dot(h_ref[...], wi_ref[0, :, D_FF + c0:D_FF + c0 + width])
        hidden = ((gate * _sigmoid(gate)) * up).astype(BF16)
        if n < len(FFN_CHUNKS) - 1:
            tie = conv_units(n * per_chunk, (n + 1) * per_chunk)
        else:
            c = conv_ref[...]
            mu = jnp.mean(c, axis=-1, keepdims=True)
            d = c - mu
            var = jnp.mean(d * d, axis=-1, keepdims=True)
            y = (d * lax.rsqrt(var + EPS)) * gln_ref[0] + bln_ref[0]
            y = y * _sigmoid(y)
            act_ref[...] = y.astype(BF16)
            tie = [y[r * SUBLANES:(r + 1) * SUBLANES, g * LANES:(g + 1) * LANES]
                   for r in range(row_groups) for g in range(CONV_CH // LANES)]
        o_ref[0:SUBLANES, 0:LANES] += zero_after(tie)
        o_ref[...] += _dot(hidden, wfo_ref[0, c0:c0 + width, :])
        c0 += width


def _post(x, a, sga, sgb, attn, layer, seq, w_dw, b_dw, g_ln, b_ln, w_co, w_ao, w_o, g_ffn, w_fi, w_fo):
    t = x.shape[0]
    tm = ROW_TILE
    n_tiles = t // tm
    per_tile = tm // HALO
    prev = lambda width: pl.BlockSpec((tm, width), lambda s: (jnp.maximum(s - 1, 0), 0))
    cur = lambda s: jnp.minimum(s, n_tiles - 1)
    return pl.pallas_call(
        functools.partial(_post_kernel, tiles_per_seq=seq // tm),
        grid=(n_tiles + 1,),
        in_specs=[prev(D_MODEL),
                  pl.BlockSpec((tm, CONV_CH), lambda s: (cur(s), 0)),
                  pl.BlockSpec((HALO, CONV_CH), lambda s: (jnp.maximum(cur(s) * per_tile - 1, 0), 0)),
                  prev(D_MODEL), prev(D_MODEL), prev(ATTN_WIDTH),
                  _layer_spec((HALO, CONV_CH), layer), _layer_spec((1, CONV_CH), layer),
                  _layer_spec((1, CONV_CH), layer), _layer_spec((1, CONV_CH), layer),
                  _layer_spec((CONV_CH, D_MODEL), layer), _layer_spec((ATTN_WIDTH, D_MODEL), layer),
                  _layer_spec((D_MODEL, D_MODEL), layer), _layer_spec((1, D_MODEL), layer),
                  _layer_spec((D_MODEL, 2 * D_FF), layer), _layer_spec((D_FF, D_MODEL), layer)],
        out_specs=prev(D_MODEL),
        out_shape=jax.ShapeDtypeStruct((t, D_MODEL), F32),
        scratch_shapes=[pltpu.VMEM((tm, CONV_CH), BF16),
                        pltpu.VMEM((CONV_CH // LANES, SUBLANES, HALO + tm, LANES), F32),
                        pltpu.VMEM((tm, CONV_CH), F32),
                        pltpu.VMEM((tm, D_MODEL), BF16)],
        compiler_params=pltpu.CompilerParams(
            dimension_semantics=("arbitrary",), vmem_limit_bytes=VMEM_LIMIT_BYTES),
        name="post",
    )(x, a, a, sga, sgb, attn, w_dw, b_dw, g_ln, b_ln, w_co, w_ao, w_o, g_ffn, w_fi, w_fo)


def kernel(x, g_mix, w_in, b_forget, w_dw, b_dw, g_conv_ln, b_conv_ln, w_conv_out,
           g_q, g_k, w_attn_out, w_out, g_ffn, w_ffn_in, w_ffn_out):
    bsz, seq, d = x.shape
    depth = w_in.shape[0]
    f0 = 2 * CONV_CH + 3 * ATTN_WIDTH

    def spread(f):
        gap = jnp.zeros(f.shape[:-1] + (F_STRIDE - N_HEADS,), f.dtype)
        tail = jnp.zeros(f.shape[:-1] + (F_PAD - F_COPIES * F_STRIDE,), f.dtype)
        return jnp.concatenate([f, gap] * F_COPIES + [tail], axis=-1)

    w_all = jnp.concatenate(
        [w_in[:, :, :f0], w_in[:, :, f0 + N_HEADS:], spread(w_in[:, :, f0:f0 + N_HEADS])], axis=2).astype(BF16)
    bf_pad = spread(b_forget)[:, None, :]
    gq = (jnp.tile(g_q, (1, N_HEADS)) * (HEAD_DIM ** -0.5 * LOG2E))[:, None, :]
    gk = jnp.tile(g_k, (1, N_HEADS))[:, None, :]
    w_dw_pad = jnp.pad(w_dw, ((0, 0), (0, HALO - CONV_WIDTH), (0, 0)))
    vec = lambda p: p[:, None, :]
    w_co, w_ao, w_o = w_conv_out.astype(BF16), w_attn_out.astype(BF16), w_out.astype(BF16)
    w_fi, w_fo = w_ffn_in.astype(BF16), w_ffn_out.astype(BF16)

    t = bsz * seq
    for l in range(depth):
        a, qa, ka, v, sga, sgb = _in_proj(x, l, vec(g_mix), w_all, bf_pad, gq, gk)
        attn = _attention(qa, ka, v)
        x = _post(x.reshape(t, d), a.reshape(t, CONV_CH), sga.reshape(t, d), sgb.reshape(t, d),
                  attn.reshape(t, ATTN_WIDTH), l, seq, w_dw_pad, vec(b_dw), vec(g_conv_ln), vec(b_conv_ln),
                  w_co, w_ao, w_o, vec(g_ffn), w_fi, w_fo).reshape(bsz, seq, d)
    return x
```

```python
import functools

import jax
import jax.numpy as jnp
import numpy as np
from jax import lax
from jax.experimental import pallas as pl
from jax.experimental.pallas import tpu as pltpu

D_MODEL = 1024
CONV_CH = 512
CONV_WIDTH = 31
N_HEADS = 8
HEAD_DIM = 64
ATTN_WIDTH = N_HEADS * HEAD_DIM
D_FF = 2816
EPS = 1e-6
LOG2E = 1.4426950408889634

LANES = 128
SUBLANES = 8
HALO = 32
F_PAD = LANES
F_COPIES = 3
F_STRIDE = 16
ONES_LANE = N_HEADS
IN_COLS_PADDED = 2 * CONV_CH + 3 * ATTN_WIDTH + 2 * D_MODEL + F_PAD
VMEM_LIMIT_BYTES = 56 * 1024 * 1024

ROW_TILE = 512
CUMSUM_CHUNK = 256
Q_TILE = 1024
KV_TILE = 512
FFN_CHUNKS = (512, 512, 512, 512, 512, 256)

BF16 = jnp.bfloat16
F32 = jnp.float32


def _dot(a, b):
    return jnp.dot(a, b, preferred_element_type=F32)


def _split3(x):
    hi = x.astype(BF16)
    r = x - hi.astype(F32)
    mid = r.astype(BF16)
    lo = (r - mid.astype(F32)).astype(BF16)
    return hi, mid, lo


def _sigmoid(x):
    return 1.0 / (1.0 + jnp.exp(-x))


def _const_spec(shape):
    nd = len(shape)
    return pl.BlockSpec(shape, lambda *_: (0,) * nd, pipeline_mode=pl.Buffered(1))


def _layer_spec(shape, layer):
    nd = len(shape)
    return pl.BlockSpec((1,) + tuple(shape), lambda *_: (layer,) + (0,) * nd, pipeline_mode=pl.Buffered(1))


def _head_rmsnorm(p, gain):
    rows = p.shape[0]
    low = lax.broadcasted_iota(jnp.int32, (rows, LANES), 1) < HEAD_DIM
    out = []
    for g in range(p.shape[1] // LANES):
        blk = p[:, g * LANES:(g + 1) * LANES]
        sq = blk * blk
        s_low = jnp.sum(jnp.where(low, sq, 0.0), axis=1, keepdims=True)
        s_high = jnp.sum(jnp.where(low, 0.0, sq), axis=1, keepdims=True)
        ms = jnp.where(low, s_low, s_high) * (1.0 / HEAD_DIM)
        out.append(blk * lax.rsqrt(ms + EPS))
    return jnp.concatenate(out, axis=1) * gain


def _in_proj_kernel(x_ref, g_ref, w_ref, bf_ref, gq_ref, gk_ref, tri_ref, place_ref,
                    a_ref, qa_ref, ka_ref, v_ref, sga_ref, sgb_ref, carry_ref):
    tm = x_ref.shape[1]

    @pl.when(pl.program_id(1) == 0)
    def _():
        carry_ref[...] = jnp.zeros_like(carry_ref)

    x = x_ref[0]
    inv = lax.rsqrt(jnp.mean(x * x, axis=-1, keepdims=True) + EPS)
    h = ((x * inv) * g_ref[0]).astype(BF16)

    col_a, col_q = 0, 2 * CONV_CH
    col_k, col_v = col_q + ATTN_WIDTH, col_q + 2 * ATTN_WIDTH
    col_ga = col_v + ATTN_WIDTH
    col_gb, col_f = col_ga + D_MODEL, col_ga + 2 * D_MODEL

    z = _dot(h, w_ref[0, :, col_f:col_f + F_PAD]) + bf_ref[0]
    pq = _dot(h, w_ref[0, :, col_q:col_q + ATTN_WIDTH])
    lane = lax.broadcasted_iota(jnp.int32, (tm, F_PAD), 1)
    valid = (lane < F_COPIES * F_STRIDE) & ((lane & N_HEADS) == 0)
    logf = jnp.where(valid, jnp.minimum(z, 0.0) - jnp.log1p(jnp.exp(-jnp.abs(z))), 0.0)
    hi, mid, lo = _split3(logf)
    parts = jnp.concatenate([hi, mid, lo], axis=1)
    carry = carry_ref[0:1, :]
    chunks = []
    for ch in range(tm // CUMSUM_CHUNK):
        y = _dot(tri_ref[...], parts[ch * CUMSUM_CHUNK:(ch + 1) * CUMSUM_CHUNK])
        c = (y[:, :F_PAD] + y[:, F_PAD:2 * F_PAD]) + y[:, 2 * F_PAD:] + carry
        carry = c[CUMSUM_CHUNK - 1:CUMSUM_CHUNK, :]
        chunks.append(c)
    cum = jnp.concatenate(chunks, axis=0)
    carry_ref[...] = jnp.broadcast_to(carry, carry_ref.shape)
    pk = _dot(h, w_ref[0, :, col_k:col_k + ATTN_WIDTH])

    cum1 = jnp.where(lane == ONES_LANE, 1.0, cum * LOG2E)
    hi, mid, lo = _split3(cum1)
    packed = jnp.where(lane < F_STRIDE, hi.astype(F32),
                       jnp.where(lane < 2 * F_STRIDE, mid.astype(F32), lo.astype(F32))).astype(BF16)
    aug = _dot(packed, place_ref[...])
    qn = _head_rmsnorm(pq, gq_ref[0])
    kn = _head_rmsnorm(pk, gk_ref[0])
    low = lax.broadcasted_iota(jnp.int32, (tm, LANES), 1) < HEAD_DIM
    for pr in range(N_HEADS // 2):
        pair = slice(pr * LANES, (pr + 1) * LANES)
        kpair = slice(ATTN_WIDTH + pr * LANES, ATTN_WIDTH + (pr + 1) * LANES)
        qa_ref[0, 2 * pr] = jnp.where(low, qn[:, pair], aug[:, pair]).astype(BF16)
        qa_ref[0, 2 * pr + 1] = jnp.where(low, aug[:, pair], qn[:, pair]).astype(BF16)
        ka_ref[0, 2 * pr] = jnp.where(low, kn[:, pair], aug[:, kpair]).astype(BF16)
        ka_ref[0, 2 * pr + 1] = jnp.where(low, aug[:, kpair], kn[:, pair]).astype(BF16)

    pa = _dot(h, w_ref[0, :, col_a:col_a + 2 * CONV_CH])
    a_ref[0] = pa[:, :CONV_CH] * _sigmoid(pa[:, CONV_CH:])
    v_ref[0] = _dot(h, w_ref[0, :, col_v:col_v + ATTN_WIDTH]).astype(BF16)
    sga_ref[0] = _sigmoid(_dot(h, w_ref[0, :, col_ga:col_ga + D_MODEL])).astype(BF16)
    sgb_ref[0] = _sigmoid(_dot(h, w_ref[0, :, col_gb:col_gb + D_MODEL])).astype(BF16)


def _placement_matrix():
    p = np.zeros((F_PAD, 2 * ATTN_WIDTH), np.float32)
    for hd in range(N_HEADS):
        base = (hd // 2) * LANES + (HEAD_DIM if hd % 2 == 0 else 0)
        kbase = ATTN_WIDTH + base
        for part in range(F_COPIES):
            p[part * F_STRIDE + hd, base + part] = 1.0
            p[ONES_LANE, base + 3 + part] = 1.0
            p[ONES_LANE, kbase + part] = 1.0
            p[part * F_STRIDE + hd, kbase + 3 + part] = -1.0
    return p


def _in_proj(x, layer, g, w_all, b_forget_pad, gq, gk):
    bsz, seq, _ = x.shape
    tm = ROW_TILE
    tri = jnp.asarray(np.tril(np.ones((CUMSUM_CHUNK, CUMSUM_CHUNK), np.float32)), BF16)
    place = jnp.asarray(_placement_matrix(), BF16)
    row = lambda width: pl.BlockSpec((1, tm, width), lambda b, i: (b, i, 0))
    heads = pl.BlockSpec((1, N_HEADS, tm, LANES), lambda b, i: (b, 0, i, 0))
    return pl.pallas_call(
        _in_proj_kernel,
        grid=(bsz, seq // tm),
        in_specs=[row(D_MODEL), _layer_spec((1, D_MODEL), layer), _layer_spec((D_MODEL, IN_COLS_PADDED), layer),
                  _layer_spec((1, F_PAD), layer), _layer_spec((1, ATTN_WIDTH), layer),
                  _layer_spec((1, ATTN_WIDTH), layer), _const_spec(tri.shape), _const_spec(place.shape)],
        out_specs=[row(CONV_CH), heads, heads, row(ATTN_WIDTH), row(D_MODEL), row(D_MODEL)],
        out_shape=[jax.ShapeDtypeStruct((bsz, seq, CONV_CH), F32),
                   jax.ShapeDtypeStruct((bsz, N_HEADS, seq, LANES), BF16),
                   jax.ShapeDtypeStruct((bsz, N_HEADS, seq, LANES), BF16),
                   jax.ShapeDtypeStruct((bsz, seq, ATTN_WIDTH), BF16),
                   jax.ShapeDtypeStruct((bsz, seq, D_MODEL), BF16),
                   jax.ShapeDtypeStruct((bsz, seq, D_MODEL), BF16)],
        scratch_shapes=[pltpu.VMEM((SUBLANES, F_PAD), F32)],
        compiler_params=pltpu.CompilerParams(
            dimension_semantics=("arbitrary", "arbitrary"), vmem_limit_bytes=VMEM_LIMIT_BYTES),
        name="in_proj",
    )(x, g, w_all, b_forget_pad, gq, gk, tri, place)


def _attn_kernel(qa_ref, ka_ref, v_ref, o_ref, vext_ref, m_ref, acc_ref, sa_ref, sb_ref):
    tq = qa_ref.shape[2]
    tk = KV_TILE
    qi = pl.program_id(2)
    top, bottom, whole = (0, tk), (tk, tq), (0, tq)

    @pl.when(qi == 0)
    def _():
        vext_ref[:, :LANES] = v_ref[0]
        vext_ref[:, LANES:] = jnp.ones((vext_ref.shape[0], LANES), BF16)

    m_ref[...] = jnp.full(m_ref.shape, -jnp.inf, F32)
    acc_ref[...] = jnp.zeros_like(acc_ref)

    def scores(j, s_ref, rows):
        start = pl.multiple_of(j * tk, tk)
        for e in range(2):
            k = ka_ref[0, e, pl.ds(start, tk), :]
            for r0 in range(rows[0], rows[1], tk):
                r = slice(r0, r0 + tk)
                s_ref[e, r] = lax.dot_general(qa_ref[0, e, r], k, (((1,), (1,)), ((), ())),
                                              preferred_element_type=F32)

    def update(j, s_ref, rows, causal):
        start = pl.multiple_of(j * tk, tk)
        vext = vext_ref[pl.ds(start, tk), :]
        for e in range(2):
            for r0 in range(rows[0], rows[1], tk):
                r = slice(r0, r0 + tk)
                s = s_ref[e, r]
                if causal:
                    row = lax.broadcasted_iota(jnp.int32, (tk, tk), 0)
                    col = lax.broadcasted_iota(jnp.int32, (tk, tk), 1)
                    s = jnp.where(col <= row, s, -jnp.inf)
                m_prev = m_ref[e, r]
                m_new = jnp.maximum(m_prev, jnp.max(s, axis=1, keepdims=True))
                alpha = jnp.exp2(m_prev - m_new)
                p = jnp.exp2(s - jnp.concatenate([m_new] * (tk // LANES), axis=1))
                acc_ref[e, r] = jnp.concatenate([alpha, alpha], axis=1) * acc_ref[e, r] + _dot(p.astype(BF16), vext)
                m_ref[e, r] = m_new

    scores(0, sa_ref, whole)

    def pair(t, carry):
        j = 2 * t
        scores(j + 1, sb_ref, whole)
        update(j, sa_ref, whole, causal=False)
        scores(j + 2, sa_ref, whole)
        update(j + 1, sb_ref, whole, causal=False)
        return carry

    lax.fori_loop(0, qi, pair, 0)

    scores(2 * qi + 1, sb_ref, bottom)
    update(2 * qi, sa_ref, top, causal=True)
    update(2 * qi, sa_ref, bottom, causal=False)
    update(2 * qi + 1, sb_ref, bottom, causal=True)

    o0 = acc_ref[0, :, :LANES] / acc_ref[0, :, LANES:]
    o1 = acc_ref[1, :, :LANES] / acc_ref[1, :, LANES:]
    lane = lax.broadcasted_iota(jnp.int32, (tq, LANES), 1)
    o_ref[0] = jnp.where(lane < HEAD_DIM, o0, o1).astype(BF16)


def _attention(qa, ka, v):
    bsz, _, seq, _ = qa.shape
    assert Q_TILE == 2 * KV_TILE
    tq = Q_TILE
    return pl.pallas_call(
        _attn_kernel,
        grid=(bsz, N_HEADS // 2, seq // tq),
        in_specs=[pl.BlockSpec((1, 2, tq, LANES), lambda b, p, i: (b, p, i, 0)),
                  pl.BlockSpec((1, 2, seq, LANES), lambda b, p, i: (b, p, 0, 0)),
                  pl.BlockSpec((1, seq, LANES), lambda b, p, i: (b, 0, p))],
        out_specs=pl.BlockSpec((1, tq, LANES), lambda b, p, i: (b, i, p)),
        out_shape=jax.ShapeDtypeStruct((bsz, seq, ATTN_WIDTH), BF16),
        scratch_shapes=[pltpu.VMEM((seq, 2 * LANES), BF16),
                        pltpu.VMEM((2, tq, LANES), F32),
                        pltpu.VMEM((2, tq, 2 * LANES), F32),
                        pltpu.VMEM((2, tq, KV_TILE), F32),
                        pltpu.VMEM((2, tq, KV_TILE), F32)],
        compiler_params=pltpu.CompilerParams(
            dimension_semantics=("arbitrary", "arbitrary", "arbitrary"), vmem_limit_bytes=VMEM_LIMIT_BYTES),
        name="fox_attention",
    )(qa, ka, v)


def _post_kernel(x_ref, a_ref, halo_ref, sga_ref, sgb_ref, attn_ref,
                 wdw_ref, bdw_ref, gln_ref, bln_ref, wco_ref, wao_ref, wo_ref, gffn_ref, wi_ref, wfo_ref,
                 o_ref, act_ref, win_ref, conv_ref, h_ref, *, tiles_per_seq):
    ts = a_ref.shape[0]
    step = pl.program_id(0)

    @pl.when(step == 0)
    def _():
        act_ref[...] = jnp.zeros_like(act_ref)

    seq_start = (step % tiles_per_seq) == 0
    lead = HALO - (CONV_WIDTH - 1)
    shifted_rows = ts + HALO - SUBLANES
    row_groups = ts // SUBLANES
    n_units = (CONV_CH // LANES) * row_groups

    def zero_after(values):
        bits = [lax.bitcast_convert_type(v, jnp.uint32) for v in values]
        while len(bits) > 1:
            bits = [bits[i] | bits[i + 1] for i in range(0, len(bits) - 1, 2)] + bits[len(bits) & ~1:]
        z = lax.shift_right_logical(lax.shift_right_logical(bits[0], jnp.uint32(16)), jnp.uint32(16))
        return lax.bitcast_convert_type(z, F32)

    def conv_units(first, last):
        accs = []
        for unit in range(first, min(last, n_units)):
            g, r = divmod(unit, row_groups)
            cols = slice(g * LANES, (g + 1) * LANES)
            if r == 0:
                win_ref[g, 0, 0:HALO, :] = jnp.where(seq_start, 0.0, halo_ref[:, cols])
                win_ref[g, 0, HALO:, :] = a_ref[:, cols]
                for k in range(1, SUBLANES):
                    win_ref[g, k, 0:shifted_rows, :] = win_ref[g, 0, k:k + shifted_rows, :]
            acc = jnp.broadcast_to(bdw_ref[0][:, cols], (SUBLANES, LANES))
            for j in range(CONV_WIDTH):
                off = lead + j
                row0 = r * SUBLANES + (off // SUBLANES) * SUBLANES
                acc = acc + wdw_ref[0, j:j + 1, cols] * win_ref[g, off % SUBLANES, row0:row0 + SUBLANES, :]
            conv_ref[r * SUBLANES:(r + 1) * SUBLANES, cols] = acc
            accs.append(acc)
        return accs

    y_conv = _dot(act_ref[...], wco_ref[0])
    y_attn = _dot(attn_ref[...], wao_ref[0])
    merged = sga_ref[...].astype(F32) * y_conv + sgb_ref[...].astype(F32) * y_attn
    x1 = x_ref[...] + _dot(merged.astype(BF16), wo_ref[0])
    inv = lax.rsqrt(jnp.mean(x1 * x1, axis=-1, keepdims=True) + EPS)
    h_ref[...] = ((x1 * inv) * gffn_ref[0]).astype(BF16)
    o_ref[...] = x1
    per_chunk = -(-n_units // (len(FFN_CHUNKS) - 1))
    c0 = 0
    for n, width in enumerate(FFN_CHUNKS):
        gate = _dot(h_ref[...], wi_ref[0, :, c0:c0 + width])
        up = _dot(h_ref[...], wi_ref[0, :, D_FF + c0:D_FF + c0 + width])
        hidden = ((gate * _sigmoid(gate)) * up).astype(BF16)
        if n < len(FFN_CHUNKS) - 1:
            tie = conv_units(n * per_chunk, (n + 1) * per_chunk)
        else:
            c = conv_ref[...]
            mu = jnp.mean(c, axis=-1, keepdims=True)
            d = c - mu
            var = jnp.mean(d * d, axis=-1, keepdims=True)
            y = (d * lax.rsqrt(var + EPS)) * gln_ref[0] + bln_ref[0]
            y = y * _sigmoid(y)
            act_ref[...] = y.astype(BF16)
            tie = [y[r * SUBLANES:(r + 1) * SUBLANES, g * LANES:(g + 1) * LANES]
                   for r in range(row_groups) for g in range(CONV_CH // LANES)]
        o_ref[0:SUBLANES, 0:LANES] += zero_after(tie)
        o_ref[...] += _dot(hidden, wfo_ref[0, c0:c0 + width, :])
        c0 += width


def _post(x, a, sga, sgb, attn, layer, seq, w_dw, b_dw, g_ln, b_ln, w_co, w_ao, w_o, g_ffn, w_fi, w_fo):
    t = x.shape[0]
    tm = ROW_TILE
    n_tiles = t // tm
    per_tile = tm // HALO
    prev = lambda width: pl.BlockSpec((tm, width), lambda s: (jnp.maximum(s - 1, 0), 0))
    cur = lambda s: jnp.minimum(s, n_tiles - 1)
    return pl.pallas_call(
        functools.partial(_post_kernel, tiles_per_seq=seq // tm),
        grid=(n_tiles + 1,),
        in_specs=[prev(D_MODEL),
                  pl.BlockSpec((tm, CONV_CH), lambda s: (cur(s), 0)),
                  pl.BlockSpec((HALO, CONV_CH), lambda s: (jnp.maximum(cur(s) * per_tile - 1, 0), 0)),
                  prev(D_MODEL), prev(D_MODEL), prev(ATTN_WIDTH),
                  _layer_spec((HALO, CONV_CH), layer), _layer_spec((1, CONV_CH), layer),
                  _layer_spec((1, CONV_CH), layer), _layer_spec((1, CONV_CH), layer),
                  _layer_spec((CONV_CH, D_MODEL), layer), _layer_spec((ATTN_WIDTH, D_MODEL), layer),
                  _layer_spec((D_MODEL, D_MODEL), layer), _layer_spec((1, D_MODEL), layer),
                  _layer_spec((D_MODEL, 2 * D_FF), layer), _layer_spec((D_FF, D_MODEL), layer)],
        out_specs=prev(D_MODEL),
        out_shape=jax.ShapeDtypeStruct((t, D_MODEL), F32),
        scratch_shapes=[pltpu.VMEM((tm, CONV_CH), BF16),
                        pltpu.VMEM((CONV_CH // LANES, SUBLANES, HALO + tm, LANES), F32),
                        pltpu.VMEM((tm, CONV_CH), F32),
                        pltpu.VMEM((tm, D_MODEL), BF16)],
        compiler_params=pltpu.CompilerParams(
            dimension_semantics=("arbitrary",), vmem_limit_bytes=VMEM_LIMIT_BYTES),
        name="post",
    )(x, a, a, sga, sgb, attn, w_dw, b_dw, g_ln, b_ln, w_co, w_ao, w_o, g_ffn, w_fi, w_fo)


def kernel(x, g_mix, w_in, b_forget, w_dw, b_dw, g_conv_ln, b_conv_ln, w_conv_out,
           g_q, g_k, w_attn_out, w_out, g_ffn, w_ffn_in, w_ffn_out):
    bsz, seq, d = x.shape
    depth = w_in.shape[0]
    f0 = 2 * CONV_CH + 3 * ATTN_WIDTH

    def spread(f):
        gap = jnp.zeros(f.shape[:-1] + (F_STRIDE - N_HEADS,), f.dtype)
        tail = jnp.zeros(f.shape[:-1] + (F_PAD - F_COPIES * F_STRIDE,), f.dtype)
        return jnp.concatenate([f, gap] * F_COPIES + [tail], axis=-1)

    w_all = jnp.concatenate(
        [w_in[:, :, :f0], w_in[:, :, f0 + N_HEADS:], spread(w_in[:, :, f0:f0 + N_HEADS])], axis=2).astype(BF16)
    bf_pad = spread(b_forget)[:, None, :]
    gq = (jnp.tile(g_q, (1, N_HEADS)) * (HEAD_DIM ** -0.5 * LOG2E))[:, None, :]
    gk = jnp.tile(g_k, (1, N_HEADS))[:, None, :]
    w_dw_pad = jnp.pad(w_dw, ((0, 0), (0, HALO - CONV_WIDTH), (0, 0)))
    vec = lambda p: p[:, None, :]
    w_co, w_ao, w_o = w_conv_out.astype(BF16), w_attn_out.astype(BF16), w_out.astype(BF16)
    w_fi, w_fo = w_ffn_in.astype(BF16), w_ffn_out.astype(BF16)

    t = bsz * seq
    for l in range(depth):
        a, qa, ka, v, sga, sgb = _in_proj(x, l, vec(g_mix), w_all, bf_pad, gq, gk)
        attn = _attention(qa, ka, v)
        x = _post(x.reshape(t, d), a.reshape(t, CONV_CH), sga.reshape(t, d), sgb.reshape(t, d),
                  attn.reshape(t, ATTN_WIDTH), l, seq, w_dw_pad, vec(b_dw), vec(g_conv_ln), vec(b_conv_ln),
                  w_co, w_ao, w_o, vec(g_ffn), w_fi, w_fo).reshape(bsz, seq, d)
    return x
```

```python
import functools

import jax
import jax.numpy as jnp
import numpy as np
from jax import lax
from jax.experimental import pallas as pl
from jax.experimental.pallas import tpu as pltpu

D_MODEL = 1024
CONV_CH = 512
CONV_WIDTH = 31
N_HEADS = 8
HEAD_DIM = 64
ATTN_WIDTH = N_HEADS * HEAD_DIM
D_FF = 2816
EPS = 1e-6
LOG2E = 1.4426950408889634

LANES = 128
SUBLANES = 8
HALO = 32
F_PAD = LANES
F_COPIES = 3
F_STRIDE = 16
ONES_LANE = N_HEADS
IN_COLS_PADDED = 2 * CONV_CH + 3 * ATTN_WIDTH + 2 * D_MODEL + F_PAD
VMEM_LIMIT_BYTES = 56 * 1024 * 1024

ROW_TILE = 512
CUMSUM_CHUNK = 256
Q_TILE = 1024
KV_TILE = 512
FFN_CHUNKS = (512, 512, 512, 512, 512, 256)

BF16 = jnp.bfloat16
F32 = jnp.float32


def _dot(a, b):
    return jnp.dot(a, b, preferred_element_type=F32)


def _split3(x):
    hi = x.astype(BF16)
    r = x - hi.astype(F32)
    mid = r.astype(BF16)
    lo = (r - mid.astype(F32)).astype(BF16)
    return hi, mid, lo


def _sigmoid(x):
    return 1.0 / (1.0 + jnp.exp(-x))


def _const_spec(shape):
    nd = len(shape)
    return pl.BlockSpec(shape, lambda *_: (0,) * nd, pipeline_mode=pl.Buffered(1))


def _layer_spec(shape, layer):
    nd = len(shape)
    return pl.BlockSpec((1,) + tuple(shape), lambda *_: (layer,) + (0,) * nd, pipeline_mode=pl.Buffered(1))


def _head_rmsnorm(p, gain):
    rows = p.shape[0]
    low = lax.broadcasted_iota(jnp.int32, (rows, LANES), 1) < HEAD_DIM
    out = []
    for g in range(p.shape[1] // LANES):
        blk = p[:, g * LANES:(g + 1) * LANES]
        sq = blk * blk
        s_low = jnp.sum(jnp.where(low, sq, 0.0), axis=1, keepdims=True)
        s_high = jnp.sum(jnp.where(low, 0.0, sq), axis=1, keepdims=True)
        ms = jnp.where(low, s_low, s_high) * (1.0 / HEAD_DIM)
        out.append(blk * lax.rsqrt(ms + EPS))
    return jnp.concatenate(out, axis=1) * gain


def _in_proj_kernel(x_ref, g_ref, w_ref, bf_ref, gq_ref, gk_ref, tri_ref, place_ref,
                    a_ref, qa_ref, ka_ref, v_ref, sga_ref, sgb_ref, carry_ref):
    tm = x_ref.shape[1]

    @pl.when(pl.program_id(1) == 0)
    def _():
        carry_ref[...] = jnp.zeros_like(carry_ref)

    x = x_ref[0]
    inv = lax.rsqrt(jnp.mean(x * x, axis=-1, keepdims=True) + EPS)
    h = ((x * inv) * g_ref[0]).astype(BF16)

    col_a, col_q = 0, 2 * CONV_CH
    col_k, col_v = col_q + ATTN_WIDTH, col_q + 2 * ATTN_WIDTH
    col_ga = col_v + ATTN_WIDTH
    col_gb, col_f = col_ga + D_MODEL, col_ga + 2 * D_MODEL

    z = _dot(h, w_ref[0, :, col_f:col_f + F_PAD]) + bf_ref[0]
    pq = _dot(h, w_ref[0, :, col_q:col_q + ATTN_WIDTH])
    lane = lax.broadcasted_iota(jnp.int32, (tm, F_PAD), 1)
    valid = (lane < F_COPIES * F_STRIDE) & ((lane & N_HEADS) == 0)
    logf = jnp.where(valid, jnp.minimum(z, 0.0) - jnp.log1p(jnp.exp(-jnp.abs(z))), 0.0)
    hi, mid, lo = _split3(logf)
    parts = jnp.concatenate([hi, mid, lo], axis=1)
    carry = carry_ref[0:1, :]
    chunks = []
    for ch in range(tm // CUMSUM_CHUNK):
        y = _dot(tri_ref[...], parts[ch * CUMSUM_CHUNK:(ch + 1) * CUMSUM_CHUNK])
        c = (y[:, :F_PAD] + y[:, F_PAD:2 * F_PAD]) + y[:, 2 * F_PAD:] + carry
        carry = c[CUMSUM_CHUNK - 1:CUMSUM_CHUNK, :]
        chunks.append(c)
    cum = jnp.concatenate(chunks, axis=0)
    carry_ref[...] = jnp.broadcast_to(carry, carry_ref.shape)
    pk = _dot(h, w_ref[0, :, col_k:col_k + ATTN_WIDTH])

    cum1 = jnp.where(lane == ONES_LANE, 1.0, cum * LOG2E)
    hi, mid, lo = _split3(cum1)
    packed = jnp.where(lane < F_STRIDE, hi.astype(F32),
                       jnp.where(lane < 2 * F_STRIDE, mid.astype(F32), lo.astype(F32))).astype(BF16)
    aug = _dot(packed, place_ref[...])
    qn = _head_rmsnorm(pq, gq_ref[0])
    kn = _head_rmsnorm(pk, gk_ref[0])
    low = lax.broadcasted_iota(jnp.int32, (tm, LANES), 1) < HEAD_DIM
    for pr in range(N_HEADS // 2):
        pair = slice(pr * LANES, (pr + 1) * LANES)
        kpair = slice(ATTN_WIDTH + pr * LANES, ATTN_WIDTH + (pr + 1) * LANES)
        qa_ref[0, 2 * pr] = jnp.where(low, qn[:, pair], aug[:, pair]).astype(BF16)
        qa_ref[0, 2 * pr + 1] = jnp.where(low, aug[:, pair], qn[:, pair]).astype(BF16)
        ka_ref[0, 2 * pr] = jnp.where(low, kn[:, pair], aug[:, kpair]).astype(BF16)
        ka_ref[0, 2 * pr + 1] = jnp.where(low, aug[:, kpair], kn[:, pair]).astype(BF16)

    pa = _dot(h, w_ref[0, :, col_a:col_a + 2 * CONV_CH])
    a_ref[0] = pa[:, :CONV_CH] * _sigmoid(pa[:, CONV_CH:])
    v_ref[0] = _dot(h, w_ref[0, :, col_v:col_v + ATTN_WIDTH]).astype(BF16)
    sga_ref[0] = _sigmoid(_dot(h, w_ref[0, :, col_ga:col_ga + D_MODEL])).astype(BF16)
    sgb_ref[0] = _sigmoid(_dot(h, w_ref[0, :, col_gb:col_gb + D_MODEL])).astype(BF16)


def _placement_matrix():
    p = np.zeros((F_PAD, 2 * ATTN_WIDTH), np.float32)
    for hd in range(N_HEADS):
        base = (hd // 2) * LANES + (HEAD_DIM if hd % 2 == 0 else 0)
        kbase = ATTN_WIDTH + base
        for part in range(F_COPIES):
            p[part * F_STRIDE + hd, base + part] = 1.0
            p[ONES_LANE, base + 3 + part] = 1.0
            p[ONES_LANE, kbase + part] = 1.0
            p[part * F_STRIDE + hd, kbase + 3 + part] = -1.0
    return p


def _in_proj(x, layer, g, w_all, b_forget_pad, gq, gk):
    bsz, seq, _ = x.shape
    tm = ROW_TILE
    tri = jnp.asarray(np.tril(np.ones((CUMSUM_CHUNK, CUMSUM_CHUNK), np.float32)), BF16)
    place = jnp.asarray(_placement_matrix(), BF16)
    row = lambda width: pl.BlockSpec((1, tm, width), lambda b, i: (b, i, 0))
    heads = pl.BlockSpec((1, N_HEADS, tm, LANES), lambda b, i: (b, 0, i, 0))
    return pl.pallas_call(
        _in_proj_kernel,
        grid=(bsz, seq // tm),
        in_specs=[row(D_MODEL), _layer_spec((1, D_MODEL), layer), _layer_spec((D_MODEL, IN_COLS_PADDED), layer),
                  _layer_spec((1, F_PAD), layer), _layer_spec((1, ATTN_WIDTH), layer),
                  _layer_spec((1, ATTN_WIDTH), layer), _const_spec(tri.shape), _const_spec(place.shape)],
        out_specs=[row(CONV_CH), heads, heads, row(ATTN_WIDTH), row(D_MODEL), row(D_MODEL)],
        out_shape=[jax.ShapeDtypeStruct((bsz, seq, CONV_CH), F32),
                   jax.ShapeDtypeStruct((bsz, N_HEADS, seq, LANES), BF16),
                   jax.ShapeDtypeStruct((bsz, N_HEADS, seq, LANES), BF16),
                   jax.ShapeDtypeStruct((bsz, seq, ATTN_WIDTH), BF16),
                   jax.ShapeDtypeStruct((bsz, seq, D_MODEL), BF16),
                   jax.ShapeDtypeStruct((bsz, seq, D_MODEL), BF16)],
        scratch_shapes=[pltpu.VMEM((SUBLANES, F_PAD), F32)],
        compiler_params=pltpu.CompilerParams(
            dimension_semantics=("arbitrary", "arbitrary"), vmem_limit_bytes=VMEM_LIMIT_BYTES),
        name="in_proj",
    )(x, g, w_all, b_forget_pad, gq, gk, tri, place)


def _attn_kernel(qa_ref, ka_ref, v_ref, o_ref, vext_ref, m_ref, acc_ref, sa_ref, sb_ref):
    tq = qa_ref.shape[2]
    tk = KV_TILE
    qi = pl.program_id(2)
    top, bottom, whole = (0, tk), (tk, tq), (0, tq)

    @pl.when(qi == 0)
    def _():
        vext_ref[:, :LANES] = v_ref[0]
        vext_ref[:, LANES:] = jnp.ones((vext_ref.shape[0], LANES), BF16)

    m_ref[...] = jnp.full(m_ref.shape, -jnp.inf, F32)
    acc_ref[...] = jnp.zeros_like(acc_ref)

    def scores(j, s_ref, rows):
        start = pl.multiple_of(j * tk, tk)
        for e in range(2):
            k = ka_ref[0, e, pl.ds(start, tk), :]
            for r0 in range(rows[0], rows[1], tk):
                r = slice(r0, r0 + tk)
                s_ref[e, r] = lax.dot_general(qa_ref[0, e, r], k, (((1,), (1,)), ((), ())),
                                              preferred_element_type=F32)

    def update(j, s_ref, rows, causal):
        start = pl.multiple_of(j * tk, tk)
        vext = vext_ref[pl.ds(start, tk), :]
        for e in range(2):
            for r0 in range(rows[0], rows[1], tk):
                r = slice(r0, r0 + tk)
                s = s_ref[e, r]
                if causal:
                    row = lax.broadcasted_iota(jnp.int32, (tk, tk), 0)
                    col = lax.broadcasted_iota(jnp.int32, (tk, tk), 1)
                    s = jnp.where(col <= row, s, -jnp.inf)
                m_prev = m_ref[e, r]
                m_new = jnp.maximum(m_prev, jnp.max(s, axis=1, keepdims=True))
                alpha = jnp.exp2(m_prev - m_new)
                p = jnp.exp2(s - jnp.concatenate([m_new] * (tk // LANES), axis=1))
                acc_ref[e, r] = jnp.concatenate([alpha, alpha], axis=1) * acc_ref[e, r] + _dot(p.astype(BF16), vext)
                m_ref[e, r] = m_new

    scores(0, sa_ref, whole)

    def pair(t, carry):
        j = 2 * t
        scores(j + 1, sb_ref, whole)
        update(j, sa_ref, whole, causal=False)
        scores(j + 2, sa_ref, whole)
        update(j + 1, sb_ref, whole, causal=False)
        return carry

    lax.fori_loop(0, qi, pair, 0)

    scores(2 * qi + 1, sb_ref, bottom)
    update(2 * qi, sa_ref, top, causal=True)
    start = pl.multiple_of(2 * qi * tk, tk)
    vext2 = vext_ref[pl.ds(start, 2 * tk), :]
    r = slice(*bottom)
    row = lax.broadcasted_iota(jnp.int32, (tk, tk), 0)
    col = lax.broadcasted_iota(jnp.int32, (tk, tk), 1)
    for e in range(2):
        s = jnp.concatenate([sa_ref[e, r], jnp.where(col <= row, sb_ref[e, r], -jnp.inf)], axis=1)
        m_prev = m_ref[e, r]
        m_new = jnp.maximum(m_prev, jnp.max(s, axis=1, keepdims=True))
        alpha = jnp.exp2(m_prev - m_new)
        p = jnp.exp2(s - jnp.concatenate([m_new] * (2 * tk // LANES), axis=1))
        acc_ref[e, r] = jnp.concatenate([alpha, alpha], axis=1) * acc_ref[e, r] + _dot(p.astype(BF16), vext2)

    o0 = acc_ref[0, :, :LANES] / acc_ref[0, :, LANES:]
    o1 = acc_ref[1, :, :LANES] / acc_ref[1, :, LANES:]
    lane = lax.broadcasted_iota(jnp.int32, (tq, LANES), 1)
    o_ref[0] = jnp.where(lane < HEAD_DIM, o0, o1).astype(BF16)


def _attention(qa, ka, v):
    bsz, _, seq, _ = qa.shape
    assert Q_TILE == 2 * KV_TILE
    tq = Q_TILE
    return pl.pallas_call(
        _attn_kernel,
        grid=(bsz, N_HEADS // 2, seq // tq),
        in_specs=[pl.BlockSpec((1, 2, tq, LANES), lambda b, p, i: (b, p, i, 0)),
                  pl.BlockSpec((1, 2, seq, LANES), lambda b, p, i: (b, p, 0, 0)),
                  pl.BlockSpec((1, seq, LANES), lambda b, p, i: (b, 0, p))],
        out_specs=pl.BlockSpec((1, tq, LANES), lambda b, p, i: (b, i, p)),
        out_shape=jax.ShapeDtypeStruct((bsz, seq, ATTN_WIDTH), BF16),
        scratch_shapes=[pltpu.VMEM((seq, 2 * LANES), BF16),
                        pltpu.VMEM((2, tq, LANES), F32),
                        pltpu.VMEM((2, tq, 2 * LANES), F32),
                        pltpu.VMEM((2, tq, KV_TILE), F32),
                        pltpu.VMEM((2, tq, KV_TILE), F32)],
        compiler_params=pltpu.CompilerParams(
            dimension_semantics=("arbitrary", "arbitrary", "arbitrary"), vmem_limit_bytes=VMEM_LIMIT_BYTES),
        name="fox_attention",
    )(qa, ka, v)


def _post_kernel(x_ref, a_ref, halo_ref, sga_ref, sgb_ref, attn_ref,
                 wdw_ref, bdw_ref, gln_ref, bln_ref, wco_ref, wao_ref, wo_ref, gffn_ref, wi_ref, wfo_ref,
                 o_ref, act_ref, win_ref, conv_ref, h_ref, *, tiles_per_seq):
    ts = a_ref.shape[0]
    step = pl.program_id(0)

    @pl.when(step == 0)
    def _():
        act_ref[...] = jnp.zeros_like(act_ref)

    seq_start = (step % tiles_per_seq) == 0
    lead = HALO - (CONV_WIDTH - 1)
    shifted_rows = ts + HALO - SUBLANES
    row_groups = ts // SUBLANES
    n_units = (CONV_CH // LANES) * row_groups

    def zero_after(values):
        bits = [lax.bitcast_convert_type(v, jnp.uint32) for v in values]
        while len(bits) > 1:
            bits = [bits[i] | bits[i + 1] for i in range(0, len(bits) - 1, 2)] + bits[len(bits) & ~1:]
        z = lax.shift_right_logical(lax.shift_right_logical(bits[0], jnp.uint32(16)), jnp.uint32(16))
        return lax.bitcast_convert_type(z, F32)

    def conv_units(first, last):
        accs = []
        for unit in range(first, min(last, n_units)):
            g, r = divmod(unit, row_groups)
            cols = slice(g * LANES, (g + 1) * LANES)
            if r == 0:
                win_ref[g, 0, 0:HALO, :] = jnp.where(seq_start, 0.0, halo_ref[:, cols])
                win_ref[g, 0, HALO:, :] = a_ref[:, cols]
                for k in range(1, SUBLANES):
                    win_ref[g, k, 0:shifted_rows, :] = win_ref[g, 0, k:k + shifted_rows, :]
            acc = jnp.broadcast_to(bdw_ref[0][:, cols], (SUBLANES, LANES))
            for j in range(CONV_WIDTH):
                off = lead + j
                row0 = r * SUBLANES + (off // SUBLANES) * SUBLANES
                acc = acc + wdw_ref[0, j:j + 1, cols] * win_ref[g, off % SUBLANES, row0:row0 + SUBLANES, :]
            conv_ref[r * SUBLANES:(r + 1) * SUBLANES, cols] = acc
            accs.append(acc)
        return accs

    y_conv = _dot(act_ref[...], wco_ref[0])
    y_attn = _dot(attn_ref[...], wao_ref[0])
    merged = sga_ref[...].astype(F32) * y_conv + sgb_ref[...].astype(F32) * y_attn
    x1 = x_ref[...] + _dot(merged.astype(BF16), wo_ref[0])
    inv = lax.rsqrt(jnp.mean(x1 * x1, axis=-1, keepdims=True) + EPS)
    h_ref[...] = ((x1 * inv) * gffn_ref[0]).astype(BF16)
    o_ref[...] = x1
    per_chunk = -(-n_units // (len(FFN_CHUNKS) - 1))
    c0 = 0
    for n, width in enumerate(FFN_CHUNKS):
        gate = _dot(h_ref[...], wi_ref[0, :, c0:c0 + width])
        up = _dot(h_ref[...], wi_ref[0, :, D_FF + c0:D_FF + c0 + width])
        hidden = ((gate * _sigmoid(gate)) * up).astype(BF16)
        if n < len(FFN_CHUNKS) - 1:
            tie = conv_units(n * per_chunk, (n + 1) * per_chunk)
        else:
            c = conv_ref[...]
            mu = jnp.mean(c, axis=-1, keepdims=True)
            d = c - mu
            var = jnp.mean(d * d, axis=-1, keepdims=True)
            y = (d * lax.rsqrt(var + EPS)) * gln_ref[0] + bln_ref[0]
            y = y * _sigmoid(y)
            act_ref[...] = y.astype(BF16)
            tie = [y[r * SUBLANES:(r + 1) * SUBLANES, g * LANES:(g + 1) * LANES]
                   for r in range(row_groups) for g in range(CONV_CH // LANES)]
        o_ref[0:SUBLANES, 0:LANES] += zero_after(tie)
        o_ref[...] += _dot(hidden, wfo_ref[0, c0:c0 + width, :])
        c0 += width


def _post(x, a, sga, sgb, attn, layer, seq, w_dw, b_dw, g_ln, b_ln, w_co, w_ao, w_o, g_ffn, w_fi, w_fo):
    t = x.shape[0]
    tm = ROW_TILE
    n_tiles = t // tm
    per_tile = tm // HALO
    prev = lambda width: pl.BlockSpec((tm, width), lambda s: (jnp.maximum(s - 1, 0), 0))
    cur = lambda s: jnp.minimum(s, n_tiles - 1)
    return pl.pallas_call(
        functools.partial(_post_kernel, tiles_per_seq=seq // tm),
        grid=(n_tiles + 1,),
        in_specs=[prev(D_MODEL),
                  pl.BlockSpec((tm, CONV_CH), lambda s: (cur(s), 0)),
                  pl.BlockSpec((HALO, CONV_CH), lambda s: (jnp.maximum(cur(s) * per_tile - 1, 0), 0)),
                  prev(D_MODEL), prev(D_MODEL), prev(ATTN_WIDTH),
                  _layer_spec((HALO, CONV_CH), layer), _layer_spec((1, CONV_CH), layer),
                  _layer_spec((1, CONV_CH), layer), _layer_spec((1, CONV_CH), layer),
                  _layer_spec((CONV_CH, D_MODEL), layer), _layer_spec((ATTN_WIDTH, D_MODEL), layer),
                  _layer_spec((D_MODEL, D_MODEL), layer), _layer_spec((1, D_MODEL), layer),
                  _layer_spec((D_MODEL, 2 * D_FF), layer), _layer_spec((D_FF, D_MODEL), layer)],
        out_specs=prev(D_MODEL),
        out_shape=jax.ShapeDtypeStruct((t, D_MODEL), F32),
        scratch_shapes=[pltpu.VMEM((tm, CONV_CH), BF16),
                        pltpu.VMEM((CONV_CH // LANES, SUBLANES, HALO + tm, LANES), F32),
                        pltpu.VMEM((tm, CONV_CH), F32),
                        pltpu.VMEM((tm, D_MODEL), BF16)],
        compiler_params=pltpu.CompilerParams(
            dimension_semantics=("arbitrary",), vmem_limit_bytes=VMEM_LIMIT_BYTES),
        name="post",
    )(x, a, a, sga, sgb, attn, w_dw, b_dw, g_ln, b_ln, w_co, w_ao, w_o, g_ffn, w_fi, w_fo)


PACK_ROWS = 256


def _pack_w_in_kernel(w_ref, o_ref):
    f0 = 2 * CONV_CH + 3 * ATTN_WIDTH
    gates = 2 * D_MODEL
    o_ref[0, :, 0:f0] = w_ref[0, :, 0:f0].astype(BF16)
    o_ref[0, :, f0:f0 + gates] = w_ref[0, :, f0 + N_HEADS:f0 + N_HEADS + gates].astype(BF16)
    blk = w_ref[0, :, f0:f0 + F_PAD]
    lane = lax.broadcasted_iota(jnp.int32, blk.shape, 1)
    f = jnp.where(lane < N_HEADS, blk, 0.0)
    spread = f
    for c in range(1, F_COPIES):
        spread = spread + pltpu.roll(f, c * F_STRIDE, 1)
    o_ref[0, :, f0 + gates:] = spread.astype(BF16)


def _pack_w_in(w_in):
    depth, d, cols = w_in.shape
    return pl.pallas_call(
        _pack_w_in_kernel,
        grid=(depth, d // PACK_ROWS),
        in_specs=[pl.BlockSpec((1, PACK_ROWS, cols), lambda l, i: (l, i, 0))],
        out_specs=pl.BlockSpec((1, PACK_ROWS, IN_COLS_PADDED), lambda l, i: (l, i, 0)),
        out_shape=jax.ShapeDtypeStruct((depth, d, IN_COLS_PADDED), BF16),
        compiler_params=pltpu.CompilerParams(
            dimension_semantics=("parallel", "parallel"), vmem_limit_bytes=VMEM_LIMIT_BYTES),
        name="pack_w_in",
    )(w_in)


def kernel(x, g_mix, w_in, b_forget, w_dw, b_dw, g_conv_ln, b_conv_ln, w_conv_out,
           g_q, g_k, w_attn_out, w_out, g_ffn, w_ffn_in, w_ffn_out):
    bsz, seq, d = x.shape
    depth = w_in.shape[0]

    def spread(f):
        gap = jnp.zeros(f.shape[:-1] + (F_STRIDE - N_HEADS,), f.dtype)
        tail = jnp.zeros(f.shape[:-1] + (F_PAD - F_COPIES * F_STRIDE,), f.dtype)
        return jnp.concatenate([f, gap] * F_COPIES + [tail], axis=-1)

    w_all = _pack_w_in(w_in)
    bf_pad = spread(b_forget)[:, None, :]
    gq = (jnp.tile(g_q, (1, N_HEADS)) * (HEAD_DIM ** -0.5 * LOG2E))[:, None, :]
    gk = jnp.tile(g_k, (1, N_HEADS))[:, None, :]
    w_dw_pad = jnp.pad(w_dw, ((0, 0), (0, HALO - CONV_WIDTH), (0, 0)))
    vec = lambda p: p[:, None, :]
    w_co, w_ao, w_o = w_conv_out.astype(BF16), w_attn_out.astype(BF16), w_out.astype(BF16)
    w_fi, w_fo = w_ffn_in.astype(BF16), w_ffn_out.astype(BF16)

    t = bsz * seq
    for l in range(depth):
        a, qa, ka, v, sga, sgb = _in_proj(x, l, vec(g_mix), w_all, bf_pad, gq, gk)
        attn = _attention(qa, ka, v)
        x = _post(x.reshape(t, d), a.reshape(t, CONV_CH), sga.reshape(t, d), sgb.reshape(t, d),
                  attn.reshape(t, ATTN_WIDTH), l, seq, w_dw_pad, vec(b_dw), vec(g_conv_ln), vec(b_conv_ln),
                  w_co, w_ao, w_o, vec(g_ffn), w_fi, w_fo).reshape(bsz, seq, d)
    return x
```

```python
import functools

import jax
import jax.numpy as jnp
import numpy as np
from jax import lax
from jax.experimental import pallas as pl
from jax.experimental.pallas import tpu as pltpu

D_MODEL = 1024
CONV_CH = 512
CONV_WIDTH = 31
N_HEADS = 8
HEAD_DIM = 64
ATTN_WIDTH = N_HEADS * HEAD_DIM
D_FF = 2816
EPS = 1e-6
LOG2E = 1.4426950408889634

LANES = 128
SUBLANES = 8
HALO = 32
F_PAD = LANES
F_COPIES = 3
F_STRIDE = 16
ONES_LANE = N_HEADS
IN_COLS_PADDED = 2 * CONV_CH + 3 * ATTN_WIDTH + 2 * D_MODEL + F_PAD
VMEM_LIMIT_BYTES = 56 * 1024 * 1024

ROW_TILE = 512
CUMSUM_CHUNK = 256
Q_TILE = 1024
KV_TILE = 512
FFN_CHUNKS = (512, 512, 512, 512, 512, 256)

BF16 = jnp.bfloat16
F32 = jnp.float32


def _dot(a, b):
    return jnp.dot(a, b, preferred_element_type=F32)


def _split3(x):
    hi = x.astype(BF16)
    r = x - hi.astype(F32)
    mid = r.astype(BF16)
    lo = (r - mid.astype(F32)).astype(BF16)
    return hi, mid, lo


def _sigmoid(x):
    return 1.0 / (1.0 + jnp.exp(-x))


def _const_spec(shape):
    nd = len(shape)
    return pl.BlockSpec(shape, lambda *_: (0,) * nd, pipeline_mode=pl.Buffered(1))


def _layer_spec(shape, layer):
    nd = len(shape)
    return pl.BlockSpec((1,) + tuple(shape), lambda *_: (layer,) + (0,) * nd, pipeline_mode=pl.Buffered(1))


def _head_rmsnorm(p, gain):
    rows = p.shape[0]
    low = lax.broadcasted_iota(jnp.int32, (rows, LANES), 1) < HEAD_DIM
    out = []
    for g in range(p.shape[1] // LANES):
        blk = p[:, g * LANES:(g + 1) * LANES]
        sq = blk * blk
        s_low = jnp.sum(jnp.where(low, sq, 0.0), axis=1, keepdims=True)
        s_high = jnp.sum(jnp.where(low, 0.0, sq), axis=1, keepdims=True)
        ms = jnp.where(low, s_low, s_high) * (1.0 / HEAD_DIM)
        out.append(blk * lax.rsqrt(ms + EPS))
    return jnp.concatenate(out, axis=1) * gain


def _in_proj_kernel(x_ref, g_ref, w_ref, bf_ref, gq_ref, gk_ref, tri_ref, place_ref,
                    a_ref, qa_ref, ka_ref, v_ref, sga_ref, sgb_ref, carry_ref):
    tm = x_ref.shape[1]

    @pl.when(pl.program_id(1) == 0)
    def _():
        carry_ref[...] = jnp.zeros_like(carry_ref)

    x = x_ref[0]
    inv = lax.rsqrt(jnp.mean(x * x, axis=-1, keepdims=True) + EPS)
    h = ((x * inv) * g_ref[0]).astype(BF16)

    col_a, col_q = 0, 2 * CONV_CH
    col_k, col_v = col_q + ATTN_WIDTH, col_q + 2 * ATTN_WIDTH
    col_ga = col_v + ATTN_WIDTH
    col_gb, col_f = col_ga + D_MODEL, col_ga + 2 * D_MODEL

    z = _dot(h, w_ref[0, :, col_f:col_f + F_PAD]) + bf_ref[0]
    pq = _dot(h, w_ref[0, :, col_q:col_q + ATTN_WIDTH])
    lane = lax.broadcasted_iota(jnp.int32, (tm, F_PAD), 1)
    valid = (lane < F_COPIES * F_STRIDE) & ((lane & N_HEADS) == 0)
    logf = jnp.where(valid, jnp.minimum(z, 0.0) - jnp.log1p(jnp.exp(-jnp.abs(z))), 0.0)
    hi, mid, lo = _split3(logf)
    parts = jnp.concatenate([hi, mid, lo], axis=1)
    carry = carry_ref[0:1, :]
    chunks = []
    for ch in range(tm // CUMSUM_CHUNK):
        y = _dot(tri_ref[...], parts[ch * CUMSUM_CHUNK:(ch + 1) * CUMSUM_CHUNK])
        c = (y[:, :F_PAD] + y[:, F_PAD:2 * F_PAD]) + y[:, 2 * F_PAD:] + carry
        carry = c[CUMSUM_CHUNK - 1:CUMSUM_CHUNK, :]
        chunks.append(c)
    cum = jnp.concatenate(chunks, axis=0)
    carry_ref[...] = jnp.broadcast_to(carry, carry_ref.shape)
    pk = _dot(h, w_ref[0, :, col_k:col_k + ATTN_WIDTH])

    cum1 = jnp.where(lane == ONES_LANE, 1.0, cum * LOG2E)
    hi, mid, lo = _split3(cum1)
    packed = jnp.where(lane < F_STRIDE, hi.astype(F32),
                       jnp.where(lane < 2 * F_STRIDE, mid.astype(F32), lo.astype(F32))).astype(BF16)
    aug = _dot(packed, place_ref[...])
    qn = _head_rmsnorm(pq, gq_ref[0])
    kn = _head_rmsnorm(pk, gk_ref[0])
    low = lax.broadcasted_iota(jnp.int32, (tm, LANES), 1) < HEAD_DIM
    for pr in range(N_HEADS // 2):
        pair = slice(pr * LANES, (pr + 1) * LANES)
        kpair = slice(ATTN_WIDTH + pr * LANES, ATTN_WIDTH + (pr + 1) * LANES)
        qa_ref[0, 2 * pr] = jnp.where(low, qn[:, pair], aug[:, pair]).astype(BF16)
        qa_ref[0, 2 * pr + 1] = jnp.where(low, aug[:, pair], qn[:, pair]).astype(BF16)
        ka_ref[0, 2 * pr] = jnp.where(low, kn[:, pair], aug[:, kpair]).astype(BF16)
        ka_ref[0, 2 * pr + 1] = jnp.where(low, aug[:, kpair], kn[:, pair]).astype(BF16)

    pa = _dot(h, w_ref[0, :, col_a:col_a + 2 * CONV_CH])
    a_ref[0] = pa[:, :CONV_CH] * _sigmoid(pa[:, CONV_CH:])
    v_ref[0] = _dot(h, w_ref[0, :, col_v:col_v + ATTN_WIDTH]).astype(BF16)
    sga_ref[0] = _sigmoid(_dot(h, w_ref[0, :, col_ga:col_ga + D_MODEL])).astype(BF16)
    sgb_ref[0] = _sigmoid(_dot(h, w_ref[0, :, col_gb:col_gb + D_MODEL])).astype(BF16)


def _placement_matrix():
    p = np.zeros((F_PAD, 2 * ATTN_WIDTH), np.float32)
    for hd in range(N_HEADS):
        base = (hd // 2) * LANES + (HEAD_DIM if hd % 2 == 0 else 0)
        kbase = ATTN_WIDTH + base
        for part in range(F_COPIES):
            p[part * F_STRIDE + hd, base + part] = 1.0
            p[ONES_LANE, base + 3 + part] = 1.0
            p[ONES_LANE, kbase + part] = 1.0
            p[part * F_STRIDE + hd, kbase + 3 + part] = -1.0
    return p


def _in_proj(x, layer, g, w_all, b_forget_pad, gq, gk):
    bsz, seq, _ = x.shape
    tm = ROW_TILE
    tri = jnp.asarray(np.tril(np.ones((CUMSUM_CHUNK, CUMSUM_CHUNK), np.float32)), BF16)
    place = jnp.asarray(_placement_matrix(), BF16)
    row = lambda width: pl.BlockSpec((1, tm, width), lambda b, i: (b, i, 0))
    heads = pl.BlockSpec((1, N_HEADS, tm, LANES), lambda b, i: (b, 0, i, 0))
    return pl.pallas_call(
        _in_proj_kernel,
        grid=(bsz, seq // tm),
        in_specs=[row(D_MODEL), _layer_spec((1, D_MODEL), layer), _layer_spec((D_MODEL, IN_COLS_PADDED), layer),
                  _layer_spec((1, F_PAD), layer), _layer_spec((1, ATTN_WIDTH), layer),
                  _layer_spec((1, ATTN_WIDTH), layer), _const_spec(tri.shape), _const_spec(place.shape)],
        out_specs=[row(CONV_CH), heads, heads, row(ATTN_WIDTH), row(D_MODEL), row(D_MODEL)],
        out_shape=[jax.ShapeDtypeStruct((bsz, seq, CONV_CH), F32),
                   jax.ShapeDtypeStruct((bsz, N_HEADS, seq, LANES), BF16),
                   jax.ShapeDtypeStruct((bsz, N_HEADS, seq, LANES), BF16),
                   jax.ShapeDtypeStruct((bsz, seq, ATTN_WIDTH), BF16),
                   jax.ShapeDtypeStruct((bsz, seq, D_MODEL), BF16),
                   jax.ShapeDtypeStruct((bsz, seq, D_MODEL), BF16)],
        scratch_shapes=[pltpu.VMEM((SUBLANES, F_PAD), F32)],
        compiler_params=pltpu.CompilerParams(
            dimension_semantics=("arbitrary", "arbitrary"), vmem_limit_bytes=VMEM_LIMIT_BYTES),
        name="in_proj",
    )(x, g, w_all, b_forget_pad, gq, gk, tri, place)


def _attn_kernel(qa_ref, ka_ref, v_ref, o_ref, vext_ref, m_ref, acc_ref, sa_ref, sb_ref):
    tq = qa_ref.shape[2]
    tk = KV_TILE
    qi = pl.program_id(2)
    top, bottom, whole = (0, tk), (tk, tq), (0, tq)

    @pl.when(qi == 0)
    def _():
        vext_ref[:, :LANES] = v_ref[0]
        vext_ref[:, LANES:] = jnp.ones((vext_ref.shape[0], LANES), BF16)

    m_ref[...] = jnp.full(m_ref.shape, -jnp.inf, F32)
    acc_ref[...] = jnp.zeros_like(acc_ref)

    def scores(j, s_ref, rows):
        start = pl.multiple_of(j * tk, tk)
        for e in range(2):
            k = ka_ref[0, e, pl.ds(start, tk), :]
            for r0 in range(rows[0], rows[1], tk):
                r = slice(r0, r0 + tk)
                s_ref[e, r] = lax.dot_general(qa_ref[0, e, r], k, (((1,), (1,)), ((), ())),
                                              preferred_element_type=F32)

    def update(j, s_ref, rows, causal):
        start = pl.multiple_of(j * tk, tk)
        vext = vext_ref[pl.ds(start, tk), :]
        for e in range(2):
            for r0 in range(rows[0], rows[1], tk):
                r = slice(r0, r0 + tk)
                s = s_ref[e, r]
                if causal:
                    row = lax.broadcasted_iota(jnp.int32, (tk, tk), 0)
                    col = lax.broadcasted_iota(jnp.int32, (tk, tk), 1)
                    s = jnp.where(col <= row, s, -jnp.inf)
                m_prev = m_ref[e, r]
                m_new = jnp.maximum(m_prev, jnp.max(s, axis=1, keepdims=True))
                alpha = jnp.exp2(m_prev - m_new)
                p = jnp.exp2(s - jnp.concatenate([m_new] * (tk // LANES), axis=1))
                acc_ref[e, r] = jnp.concatenate([alpha, alpha], axis=1) * acc_ref[e, r] + _dot(p.astype(BF16), vext)
                m_ref[e, r] = m_new

    scores(0, sa_ref, whole)

    def pair(t, carry):
        j = 2 * t
        scores(j + 1, sb_ref, whole)
        update(j, sa_ref, whole, causal=False)
        scores(j + 2, sa_ref, whole)
        update(j + 1, sb_ref, whole, causal=False)
        return carry

    lax.fori_loop(0, qi, pair, 0)

    scores(2 * qi + 1, sb_ref, bottom)
    update(2 * qi, sa_ref, top, causal=True)
    start = pl.multiple_of(2 * qi * tk, tk)
    vext2 = vext_ref[pl.ds(start, 2 * tk), :]
    r = slice(*bottom)
    row = lax.broadcasted_iota(jnp.int32, (tk, tk), 0)
    col = lax.broadcasted_iota(jnp.int32, (tk, tk), 1)
    for e in range(2):
        s = jnp.concatenate([sa_ref[e, r], jnp.where(col <= row, sb_ref[e, r], -jnp.inf)], axis=1)
        m_prev = m_ref[e, r]
        m_new = jnp.maximum(m_prev, jnp.max(s, axis=1, keepdims=True))
        alpha = jnp.exp2(m_prev - m_new)
        p = jnp.exp2(s - jnp.concatenate([m_new] * (2 * tk // LANES), axis=1))
        acc_ref[e, r] = jnp.concatenate([alpha, alpha], axis=1) * acc_ref[e, r] + _dot(p.astype(BF16), vext2)

    o0 = acc_ref[0, :, :LANES] / acc_ref[0, :, LANES:]
    o1 = acc_ref[1, :, :LANES] / acc_ref[1, :, LANES:]
    lane = lax.broadcasted_iota(jnp.int32, (tq, LANES), 1)
    o_ref[0] = jnp.where(lane < HEAD_DIM, o0, o1).astype(BF16)


def _attention(qa, ka, v):
    bsz, _, seq, _ = qa.shape
    assert Q_TILE == 2 * KV_TILE
    tq = Q_TILE
    return pl.pallas_call(
        _attn_kernel,
        grid=(bsz, N_HEADS // 2, seq // tq),
        in_specs=[pl.BlockSpec((1, 2, tq, LANES), lambda b, p, i: (b, p, i, 0)),
                  pl.BlockSpec((1, 2, seq, LANES), lambda b, p, i: (b, p, 0, 0)),
                  pl.BlockSpec((1, seq, LANES), lambda b, p, i: (b, 0, p))],
        out_specs=pl.BlockSpec((1, tq, LANES), lambda b, p, i: (b, i, p)),
        out_shape=jax.ShapeDtypeStruct((bsz, seq, ATTN_WIDTH), BF16),
        scratch_shapes=[pltpu.VMEM((seq, 2 * LANES), BF16),
                        pltpu.VMEM((2, tq, LANES), F32),
                        pltpu.VMEM((2, tq, 2 * LANES), F32),
                        pltpu.VMEM((2, tq, KV_TILE), F32),
                        pltpu.VMEM((2, tq, KV_TILE), F32)],
        compiler_params=pltpu.CompilerParams(
            dimension_semantics=("arbitrary", "arbitrary", "arbitrary"), vmem_limit_bytes=VMEM_LIMIT_BYTES),
        name="fox_attention",
    )(qa, ka, v)


def _post_kernel(x_ref, a_ref, halo_ref, sga_ref, sgb_ref, attn_ref,
                 wdw_ref, bdw_ref, gln_ref, bln_ref, wco_ref, wao_ref, wo_ref, gffn_ref, wi_ref, wfo_ref,
                 o_ref, act_ref, win_ref, conv_ref, h_ref, *, tiles_per_seq):
    ts = a_ref.shape[0]
    step = pl.program_id(0)

    @pl.when(step == 0)
    def _():
        act_ref[...] = jnp.zeros_like(act_ref)

    seq_start = (step % tiles_per_seq) == 0
    lead = HALO - (CONV_WIDTH - 1)
    shifted_rows = ts + HALO - SUBLANES
    row_groups = ts // SUBLANES
    n_units = (CONV_CH // LANES) * row_groups

    def zero_after(values):
        bits = [lax.bitcast_convert_type(v, jnp.uint32) for v in values]
        while len(bits) > 1:
            bits = [bits[i] | bits[i + 1] for i in range(0, len(bits) - 1, 2)] + bits[len(bits) & ~1:]
        z = lax.shift_right_logical(lax.shift_right_logical(bits[0], jnp.uint32(16)), jnp.uint32(16))
        return lax.bitcast_convert_type(z, F32)

    def conv_units(first, last):
        accs = []
        for unit in range(first, min(last, n_units)):
            g, r = divmod(unit, row_groups)
            cols = slice(g * LANES, (g + 1) * LANES)
            if r == 0:
                win_ref[g, 0, 0:HALO, :] = jnp.where(seq_start, 0.0, halo_ref[:, cols])
                win_ref[g, 0, HALO:, :] = a_ref[:, cols]
                for k in range(1, SUBLANES):
                    win_ref[g, k, 0:shifted_rows, :] = win_ref[g, 0, k:k + shifted_rows, :]
            acc = jnp.broadcast_to(bdw_ref[0][:, cols], (SUBLANES, LANES))
            for j in range(CONV_WIDTH):
                off = lead + j
                row0 = r * SUBLANES + (off // SUBLANES) * SUBLANES
                acc = acc + wdw_ref[0, j:j + 1, cols] * win_ref[g, off % SUBLANES, row0:row0 + SUBLANES, :]
            conv_ref[r * SUBLANES:(r + 1) * SUBLANES, cols] = acc
            accs.append(acc)
        return accs

    y_conv = _dot(act_ref[...], wco_ref[0])
    y_attn = _dot(attn_ref[...], wao_ref[0])
    merged = sga_ref[...].astype(F32) * y_conv + sgb_ref[...].astype(F32) * y_attn
    x1 = x_ref[...] + _dot(merged.astype(BF16), wo_ref[0])
    inv = lax.rsqrt(jnp.mean(x1 * x1, axis=-1, keepdims=True) + EPS)
    h_ref[...] = ((x1 * inv) * gffn_ref[0]).astype(BF16)
    o_ref[...] = x1
    per_chunk = -(-n_units // (len(FFN_CHUNKS) - 1))
    c0 = 0
    for n, width in enumerate(FFN_CHUNKS):
        gate = _dot(h_ref[...], wi_ref[0, :, c0:c0 + width])
        up = _dot(h_ref[...], wi_ref[0, :, D_FF + c0:D_FF + c0 + width])
        hidden = ((gate * _sigmoid(gate)) * up).astype(BF16)
        if n < len(FFN_CHUNKS) - 1:
            tie = conv_units(n * per_chunk, (n + 1) * per_chunk)
        else:
            c = conv_ref[...]
            mu = jnp.mean(c, axis=-1, keepdims=True)
            d = c - mu
            var = jnp.mean(d * d, axis=-1, keepdims=True)
            y = (d * lax.rsqrt(var + EPS)) * gln_ref[0] + bln_ref[0]
            y = y * _sigmoid(y)
            act_ref[...] = y.astype(BF16)
            tie = [y[r * SUBLANES:(r + 1) * SUBLANES, g * LANES:(g + 1) * LANES]
                   for r in range(row_groups) for g in range(CONV_CH // LANES)]
        o_ref[0:SUBLANES, 0:LANES] += zero_after(tie)
        o_ref[...] += _dot(hidden, wfo_ref[0, c0:c0 + width, :])
        c0 += width


def _post(x, a, sga, sgb, attn, layer, seq, w_dw, b_dw, g_ln, b_ln, w_co, w_ao, w_o, g_ffn, w_fi, w_fo):
    t = x.shape[0]
    tm = ROW_TILE
    n_tiles = t // tm
    per_tile = tm // HALO
    prev = lambda width: pl.BlockSpec((tm, width), lambda s: (jnp.maximum(s - 1, 0), 0))
    cur = lambda s: jnp.minimum(s, n_tiles - 1)
    return pl.pallas_call(
        functools.partial(_post_kernel, tiles_per_seq=seq // tm),
        grid=(n_tiles + 1,),
        in_specs=[prev(D_MODEL),
                  pl.BlockSpec((tm, CONV_CH), lambda s: (cur(s), 0)),
                  pl.BlockSpec((HALO, CONV_CH), lambda s: (jnp.maximum(cur(s) * per_tile - 1, 0), 0)),
                  prev(D_MODEL), prev(D_MODEL), prev(ATTN_WIDTH),
                  _layer_spec((HALO, CONV_CH), layer), _layer_spec((1, CONV_CH), layer),
                  _layer_spec((1, CONV_CH), layer), _layer_spec((1, CONV_CH), layer),
                  _layer_spec((CONV_CH, D_MODEL), layer), _layer_spec((ATTN_WIDTH, D_MODEL), layer),
                  _layer_spec((D_MODEL, D_MODEL), layer), _layer_spec((1, D_MODEL), layer),
                  _layer_spec((D_MODEL, 2 * D_FF), layer), _layer_spec((D_FF, D_MODEL), layer)],
        out_specs=prev(D_MODEL),
        out_shape=jax.ShapeDtypeStruct((t, D_MODEL), F32),
        scratch_shapes=[pltpu.VMEM((tm, CONV_CH), BF16),
                        pltpu.VMEM((CONV_CH // LANES, SUBLANES, HALO + tm, LANES), F32),
                        pltpu.VMEM((tm, CONV_CH), F32),
                        pltpu.VMEM((tm, D_MODEL), BF16)],
        compiler_params=pltpu.CompilerParams(
            dimension_semantics=("arbitrary",), vmem_limit_bytes=VMEM_LIMIT_BYTES),
        name="post",
    )(x, a, a, sga, sgb, attn, w_dw, b_dw, g_ln, b_ln, w_co, w_ao, w_o, g_ffn, w_fi, w_fo)


PACK_CHUNK = 256


def _pack_w_in_kernel(wt_ref, o_ref):
    f0 = 2 * CONV_CH + 3 * ATTN_WIDTH
    gates = 2 * D_MODEL

    def put(col0, row0, n):
        for c in range(0, n, PACK_CHUNK):
            o_ref[0, :, col0 + c:col0 + c + PACK_CHUNK] = (
                wt_ref[0, row0 + c:row0 + c + PACK_CHUNK, :].T.astype(BF16))

    put(0, 0, f0)
    put(f0, f0 + N_HEADS, gates)
    f = wt_ref[0, f0:f0 + N_HEADS, :]
    gap = jnp.zeros((F_STRIDE - N_HEADS, D_MODEL), F32)
    tail = jnp.zeros((F_PAD - F_COPIES * F_STRIDE, D_MODEL), F32)
    o_ref[0, :, f0 + gates:] = jnp.concatenate([f, gap] * F_COPIES + [tail], axis=0).T.astype(BF16)


def _pack_w_in(w_in):
    depth, d, cols = w_in.shape
    return pl.pallas_call(
        _pack_w_in_kernel,
        grid=(depth,),
        in_specs=[pl.BlockSpec((1, cols, d), lambda l: (l, 0, 0), pipeline_mode=pl.Buffered(1))],
        out_specs=pl.BlockSpec((1, d, IN_COLS_PADDED), lambda l: (l, 0, 0)),
        out_shape=jax.ShapeDtypeStruct((depth, d, IN_COLS_PADDED), BF16),
        compiler_params=pltpu.CompilerParams(
            dimension_semantics=("parallel",), vmem_limit_bytes=VMEM_LIMIT_BYTES),
        name="pack_w_in",
    )(jnp.swapaxes(w_in, 1, 2))


def kernel(x, g_mix, w_in, b_forget, w_dw, b_dw, g_conv_ln, b_conv_ln, w_conv_out,
           g_q, g_k, w_attn_out, w_out, g_ffn, w_ffn_in, w_ffn_out):
    bsz, seq, d = x.shape
    depth = w_in.shape[0]

    def spread(f):
        gap = jnp.zeros(f.shape[:-1] + (F_STRIDE - N_HEADS,), f.dtype)
        tail = jnp.zeros(f.shape[:-1] + (F_PAD - F_COPIES * F_STRIDE,), f.dtype)
        return jnp.concatenate([f, gap] * F_COPIES + [tail], axis=-1)

    w_all = _pack_w_in(w_in)
    bf_pad = spread(b_forget)[:, None, :]
    gq = (jnp.tile(g_q, (1, N_HEADS)) * (HEAD_DIM ** -0.5 * LOG2E))[:, None, :]
    gk = jnp.tile(g_k, (1, N_HEADS))[:, None, :]
    w_dw_pad = jnp.pad(w_dw, ((0, 0), (0, HALO - CONV_WIDTH), (0, 0)))
    vec = lambda p: p[:, None, :]
    w_co, w_ao, w_o = w_conv_out.astype(BF16), w_attn_out.astype(BF16), w_out.astype(BF16)
    w_fi, w_fo = w_ffn_in.astype(BF16), w_ffn_out.astype(BF16)

    t = bsz * seq
    for l in range(depth):
        a, qa, ka, v, sga, sgb = _in_proj(x, l, vec(g_mix), w_all, bf_pad, gq, gk)
        attn = _attention(qa, ka, v)
        x = _post(x.reshape(t, d), a.reshape(t, CONV_CH), sga.reshape(t, d), sgb.reshape(t, d),
                  attn.reshape(t, ATTN_WIDTH), l, seq, w_dw_pad, vec(b_dw), vec(g_conv_ln), vec(b_conv_ln),
                  w_co, w_ao, w_o, vec(g_ffn), w_fi, w_fo).reshape(bsz, seq, d)
    return x
```

```python
import functools

import jax
import jax.numpy as jnp
import numpy as np
from jax import lax
from jax.experimental import pallas as pl
from jax.experimental.pallas import tpu as pltpu

D_MODEL = 1024
CONV_CH = 512
CONV_WIDTH = 31
N_HEADS = 8
HEAD_DIM = 64
ATTN_WIDTH = N_HEADS * HEAD_DIM
D_FF = 2816
EPS = 1e-6
LOG2E = 1.4426950408889634

LANES = 128
SUBLANES = 8
HALO = 32
F_PAD = LANES
F_COPIES = 3
F_STRIDE = 16
ONES_LANE = N_HEADS
IN_COLS_PADDED = 2 * CONV_CH + 3 * ATTN_WIDTH + 2 * D_MODEL + F_PAD
VMEM_LIMIT_BYTES = 56 * 1024 * 1024

ROW_TILE = 512
CUMSUM_CHUNK = 256
Q_TILE = 1024
KV_TILE = 512
FFN_CHUNKS = (512, 512, 512, 512, 512, 256)

BF16 = jnp.bfloat16
F32 = jnp.float32


def _dot(a, b):
    return jnp.dot(a, b, preferred_element_type=F32)


def _split3(x):
    hi = x.astype(BF16)
    r = x - hi.astype(F32)
    mid = r.astype(BF16)
    lo = (r - mid.astype(F32)).astype(BF16)
    return hi, mid, lo


def _sigmoid(x):
    return 1.0 / (1.0 + jnp.exp(-x))


def _const_spec(shape):
    nd = len(shape)
    return pl.BlockSpec(shape, lambda *_: (0,) * nd, pipeline_mode=pl.Buffered(1))


def _layer_spec(shape, layer):
    nd = len(shape)
    return pl.BlockSpec((1,) + tuple(shape), lambda *_: (layer,) + (0,) * nd, pipeline_mode=pl.Buffered(1))


def _head_rmsnorm(p, gain):
    rows = p.shape[0]
    low = lax.broadcasted_iota(jnp.int32, (rows, LANES), 1) < HEAD_DIM
    out = []
    for g in range(p.shape[1] // LANES):
        blk = p[:, g * LANES:(g + 1) * LANES]
        sq = blk * blk
        s_low = jnp.sum(jnp.where(low, sq, 0.0), axis=1, keepdims=True)
        s_high = jnp.sum(jnp.where(low, 0.0, sq), axis=1, keepdims=True)
        ms = jnp.where(low, s_low, s_high) * (1.0 / HEAD_DIM)
        out.append(blk * lax.rsqrt(ms + EPS))
    return jnp.concatenate(out, axis=1) * gain


def _in_proj_kernel(x_ref, g_ref, w_ref, bf_ref, gq_ref, gk_ref, tri_ref, place_ref,
                    a_ref, qa_ref, ka_ref, v_ref, sga_ref, sgb_ref, carry_ref):
    tm = x_ref.shape[1]

    @pl.when(pl.program_id(1) == 0)
    def _():
        carry_ref[...] = jnp.zeros_like(carry_ref)

    x = x_ref[0]
    inv = lax.rsqrt(jnp.mean(x * x, axis=-1, keepdims=True) + EPS)
    h = ((x * inv) * g_ref[0]).astype(BF16)

    col_a, col_q = 0, 2 * CONV_CH
    col_k, col_v = col_q + ATTN_WIDTH, col_q + 2 * ATTN_WIDTH
    col_ga = col_v + ATTN_WIDTH
    col_gb, col_f = col_ga + D_MODEL, col_ga + 2 * D_MODEL

    z = _dot(h, w_ref[0, :, col_f:col_f + F_PAD]) + bf_ref[0]
    pq = _dot(h, w_ref[0, :, col_q:col_q + ATTN_WIDTH])
    lane = lax.broadcasted_iota(jnp.int32, (tm, F_PAD), 1)
    valid = (lane < F_COPIES * F_STRIDE) & ((lane & N_HEADS) == 0)
    logf = jnp.where(valid, jnp.minimum(z, 0.0) - jnp.log1p(jnp.exp(-jnp.abs(z))), 0.0)
    hi, mid, lo = _split3(logf)
    parts = jnp.concatenate([hi, mid, lo], axis=1)
    carry = carry_ref[0:1, :]
    chunks = []
    for ch in range(tm // CUMSUM_CHUNK):
        y = _dot(tri_ref[...], parts[ch * CUMSUM_CHUNK:(ch + 1) * CUMSUM_CHUNK])
        c = (y[:, :F_PAD] + y[:, F_PAD:2 * F_PAD]) + y[:, 2 * F_PAD:] + carry
        carry = c[CUMSUM_CHUNK - 1:CUMSUM_CHUNK, :]
        chunks.append(c)
    cum = jnp.concatenate(chunks, axis=0)
    carry_ref[...] = jnp.broadcast_to(carry, carry_ref.shape)
    pk = _dot(h, w_ref[0, :, col_k:col_k + ATTN_WIDTH])

    cum1 = jnp.where(lane == ONES_LANE, 1.0, cum * LOG2E)
    hi, mid, lo = _split3(cum1)
    packed = jnp.where(lane < F_STRIDE, hi.astype(F32),
                       jnp.where(lane < 2 * F_STRIDE, mid.astype(F32), lo.astype(F32))).astype(BF16)
    aug = _dot(packed, place_ref[...])
    qn = _head_rmsnorm(pq, gq_ref[0])
    kn = _head_rmsnorm(pk, gk_ref[0])
    low = lax.broadcasted_iota(jnp.int32, (tm, LANES), 1) < HEAD_DIM
    for pr in range(N_HEADS // 2):
        pair = slice(pr * LANES, (pr + 1) * LANES)
        kpair = slice(ATTN_WIDTH + pr * LANES, ATTN_WIDTH + (pr + 1) * LANES)
        qa_ref[0, 2 * pr] = jnp.where(low, qn[:, pair], aug[:, pair]).astype(BF16)
        qa_ref[0, 2 * pr + 1] = jnp.where(low, aug[:, pair], qn[:, pair]).astype(BF16)
        ka_ref[0, 2 * pr] = jnp.where(low, kn[:, pair], aug[:, kpair]).astype(BF16)
        ka_ref[0, 2 * pr + 1] = jnp.where(low, aug[:, kpair], kn[:, pair]).astype(BF16)

    pa = _dot(h, w_ref[0, :, col_a:col_a + 2 * CONV_CH])
    a_ref[0] = pa[:, :CONV_CH] * _sigmoid(pa[:, CONV_CH:])
    v_ref[0] = _dot(h, w_ref[0, :, col_v:col_v + ATTN_WIDTH]).astype(BF16)
    sga_ref[0] = _sigmoid(_dot(h, w_ref[0, :, col_ga:col_ga + D_MODEL])).astype(BF16)
    sgb_ref[0] = _sigmoid(_dot(h, w_ref[0, :, col_gb:col_gb + D_MODEL])).astype(BF16)


def _placement_matrix():
    p = np.zeros((F_PAD, 2 * ATTN_WIDTH), np.float32)
    for hd in range(N_HEADS):
        base = (hd // 2) * LANES + (HEAD_DIM if hd % 2 == 0 else 0)
        kbase = ATTN_WIDTH + base
        for part in range(F_COPIES):
            p[part * F_STRIDE + hd, base + part] = 1.0
            p[ONES_LANE, base + 3 + part] = 1.0
            p[ONES_LANE, kbase + part] = 1.0
            p[part * F_STRIDE + hd, kbase + 3 + part] = -1.0
    return p


def _in_proj(x, layer, g, w_all, b_forget_pad, gq, gk):
    bsz, seq, _ = x.shape
    tm = ROW_TILE
    tri = jnp.asarray(np.tril(np.ones((CUMSUM_CHUNK, CUMSUM_CHUNK), np.float32)), BF16)
    place = jnp.asarray(_placement_matrix(), BF16)
    row = lambda width: pl.BlockSpec((1, tm, width), lambda b, i: (b, i, 0))
    heads = pl.BlockSpec((1, N_HEADS, tm, LANES), lambda b, i: (b, 0, i, 0))
    return pl.pallas_call(
        _in_proj_kernel,
        grid=(bsz, seq // tm),
        in_specs=[row(D_MODEL), _layer_spec((1, D_MODEL), layer), _layer_spec((D_MODEL, IN_COLS_PADDED), layer),
                  _layer_spec((1, F_PAD), layer), _layer_spec((1, ATTN_WIDTH), layer),
                  _layer_spec((1, ATTN_WIDTH), layer), _const_spec(tri.shape), _const_spec(place.shape)],
        out_specs=[row(CONV_CH), heads, heads, row(ATTN_WIDTH), row(D_MODEL), row(D_MODEL)],
        out_shape=[jax.ShapeDtypeStruct((bsz, seq, CONV_CH), F32),
                   jax.ShapeDtypeStruct((bsz, N_HEADS, seq, LANES), BF16),
                   jax.ShapeDtypeStruct((bsz, N_HEADS, seq, LANES), BF16),
                   jax.ShapeDtypeStruct((bsz, seq, ATTN_WIDTH), BF16),
                   jax.ShapeDtypeStruct((bsz, seq, D_MODEL), BF16),
                   jax.ShapeDtypeStruct((bsz, seq, D_MODEL), BF16)],
        scratch_shapes=[pltpu.VMEM((SUBLANES, F_PAD), F32)],
        compiler_params=pltpu.CompilerParams(
            dimension_semantics=("arbitrary", "arbitrary"), vmem_limit_bytes=VMEM_LIMIT_BYTES),
        name="in_proj",
    )(x, g, w_all, b_forget_pad, gq, gk, tri, place)


def _attn_kernel(qa_ref, ka_ref, v_ref, o_ref, vext_ref, m_ref, acc_ref, sa_ref, sb_ref):
    tq = qa_ref.shape[2]
    tk = KV_TILE
    qi = pl.program_id(2)
    top, bottom, whole = (0, tk), (tk, tq), (0, tq)

    @pl.when(qi == 0)
    def _():
        vext_ref[:, :LANES] = v_ref[0]
        vext_ref[:, LANES:] = jnp.ones((vext_ref.shape[0], LANES), BF16)

    m_ref[...] = jnp.full(m_ref.shape, -jnp.inf, F32)
    acc_ref[...] = jnp.zeros_like(acc_ref)

    def scores(j, s_ref, rows):
        start = pl.multiple_of(j * tk, tk)
        for e in range(2):
            k = ka_ref[0, e, pl.ds(start, tk), :]
            for r0 in range(rows[0], rows[1], tk):
                r = slice(r0, r0 + tk)
                s_ref[e, r] = lax.dot_general(qa_ref[0, e, r], k, (((1,), (1,)), ((), ())),
                                              preferred_element_type=F32)

    def update(j, s_ref, rows, causal):
        start = pl.multiple_of(j * tk, tk)
        vext = vext_ref[pl.ds(start, tk), :]
        for e in range(2):
            for r0 in range(rows[0], rows[1], tk):
                r = slice(r0, r0 + tk)
                s = s_ref[e, r]
                if causal:
                    row = lax.broadcasted_iota(jnp.int32, (tk, tk), 0)
                    col = lax.broadcasted_iota(jnp.int32, (tk, tk), 1)
                    s = jnp.where(col <= row, s, -jnp.inf)
                m_prev = m_ref[e, r]
                m_new = jnp.maximum(m_prev, jnp.max(s, axis=1, keepdims=True))
                alpha = jnp.exp2(m_prev - m_new)
                p = jnp.exp2(s - jnp.concatenate([m_new] * (tk // LANES), axis=1))
                acc_ref[e, r] = jnp.concatenate([alpha, alpha], axis=1) * acc_ref[e, r] + _dot(p.astype(BF16), vext)
                m_ref[e, r] = m_new

    scores(0, sa_ref, whole)

    def pair(t, carry):
        j = 2 * t
        scores(j + 1, sb_ref, whole)
        update(j, sa_ref, whole, causal=False)
        scores(j + 2, sa_ref, whole)
        update(j + 1, sb_ref, whole, causal=False)
        return carry

    lax.fori_loop(0, qi, pair, 0)

    scores(2 * qi + 1, sb_ref, bottom)
    update(2 * qi, sa_ref, top, causal=True)
    start = pl.multiple_of(2 * qi * tk, tk)
    vext2 = vext_ref[pl.ds(start, 2 * tk), :]
    r = slice(*bottom)
    row = lax.broadcasted_iota(jnp.int32, (tk, tk), 0)
    col = lax.broadcasted_iota(jnp.int32, (tk, tk), 1)
    for e in range(2):
        s = jnp.concatenate([sa_ref[e, r], jnp.where(col <= row, sb_ref[e, r], -jnp.inf)], axis=1)
        m_prev = m_ref[e, r]
        m_new = jnp.maximum(m_prev, jnp.max(s, axis=1, keepdims=True))
        alpha = jnp.exp2(m_prev - m_new)
        p = jnp.exp2(s - jnp.concatenate([m_new] * (2 * tk // LANES), axis=1))
        acc_ref[e, r] = jnp.concatenate([alpha, alpha], axis=1) * acc_ref[e, r] + _dot(p.astype(BF16), vext2)

    o0 = acc_ref[0, :, :LANES] / acc_ref[0, :, LANES:]
    o1 = acc_ref[1, :, :LANES] / acc_ref[1, :, LANES:]
    lane = lax.broadcasted_iota(jnp.int32, (tq, LANES), 1)
    o_ref[0] = jnp.where(lane < HEAD_DIM, o0, o1).astype(BF16)


def _attention(qa, ka, v):
    bsz, _, seq, _ = qa.shape
    assert Q_TILE == 2 * KV_TILE
    tq = Q_TILE
    return pl.pallas_call(
        _attn_kernel,
        grid=(bsz, N_HEADS // 2, seq // tq),
        in_specs=[pl.BlockSpec((1, 2, tq, LANES), lambda b, p, i: (b, p, i, 0)),
                  pl.BlockSpec((1, 2, seq, LANES), lambda b, p, i: (b, p, 0, 0)),
                  pl.BlockSpec((1, seq, LANES), lambda b, p, i: (b, 0, p))],
        out_specs=pl.BlockSpec((1, tq, LANES), lambda b, p, i: (b, i, p)),
        out_shape=jax.ShapeDtypeStruct((bsz, seq, ATTN_WIDTH), BF16),
        scratch_shapes=[pltpu.VMEM((seq, 2 * LANES), BF16),
                        pltpu.VMEM((2, tq, LANES), F32),
                        pltpu.VMEM((2, tq, 2 * LANES), F32),
                        pltpu.VMEM((2, tq, KV_TILE), F32),
                        pltpu.VMEM((2, tq, KV_TILE), F32)],
        compiler_params=pltpu.CompilerParams(
            dimension_semantics=("arbitrary", "arbitrary", "arbitrary"), vmem_limit_bytes=VMEM_LIMIT_BYTES),
        name="fox_attention",
    )(qa, ka, v)


def _post_kernel(x_ref, a_ref, halo_ref, sga_ref, sgb_ref, attn_ref,
                 wdw_ref, bdw_ref, gln_ref, bln_ref, wco_ref, wao_ref, wo_ref, gffn_ref, wi_ref, wfo_ref,
                 o_ref, act_ref, win_ref, conv_ref, h_ref, *, tiles_per_seq):
    ts = a_ref.shape[0]
    step = pl.program_id(0)

    @pl.when(step == 0)
    def _():
        act_ref[...] = jnp.zeros_like(act_ref)

    seq_start = (step % tiles_per_seq) == 0
    lead = HALO - (CONV_WIDTH - 1)
    shifted_rows = ts + HALO - SUBLANES
    row_groups = ts // SUBLANES
    n_units = (CONV_CH // LANES) * row_groups

    def zero_after(values):
        bits = [lax.bitcast_convert_type(v, jnp.uint32) for v in values]
        while len(bits) > 1:
            bits = [bits[i] | bits[i + 1] for i in range(0, len(bits) - 1, 2)] + bits[len(bits) & ~1:]
        z = lax.shift_right_logical(lax.shift_right_logical(bits[0], jnp.uint32(16)), jnp.uint32(16))
        return lax.bitcast_convert_type(z, F32)

    def conv_units(first, last):
        accs = []
        for unit in range(first, min(last, n_units)):
            g, r = divmod(unit, row_groups)
            cols = slice(g * LANES, (g + 1) * LANES)
            if r == 0:
                win_ref[g, 0, 0:HALO, :] = jnp.where(seq_start, 0.0, halo_ref[:, cols])
                win_ref[g, 0, HALO:, :] = a_ref[:, cols]
                for k in range(1, SUBLANES):
                    win_ref[g, k, 0:shifted_rows, :] = win_ref[g, 0, k:k + shifted_rows, :]
            acc = jnp.broadcast_to(bdw_ref[0][:, cols], (SUBLANES, LANES))
            for j in range(CONV_WIDTH):
                off = lead + j
                row0 = r * SUBLANES + (off // SUBLANES) * SUBLANES
                acc = acc + wdw_ref[0, j:j + 1, cols] * win_ref[g, off % SUBLANES, row0:row0 + SUBLANES, :]
            conv_ref[r * SUBLANES:(r + 1) * SUBLANES, cols] = acc
            accs.append(acc)
        return accs

    y_conv = _dot(act_ref[...], wco_ref[0])
    y_attn = _dot(attn_ref[...], wao_ref[0])
    merged = sga_ref[...].astype(F32) * y_conv + sgb_ref[...].astype(F32) * y_attn
    x1 = x_ref[...] + _dot(merged.astype(BF16), wo_ref[0])
    inv = lax.rsqrt(jnp.mean(x1 * x1, axis=-1, keepdims=True) + EPS)
    h_ref[...] = ((x1 * inv) * gffn_ref[0]).astype(BF16)
    o_ref[...] = x1
    conv_chunks = len(FFN_CHUNKS) - 2
    per_chunk = -(-n_units // conv_chunks)
    c0 = 0
    for n, width in enumerate(FFN_CHUNKS):
        gate = _dot(h_ref[...], wi_ref[0, :, c0:c0 + width])
        up = _dot(h_ref[...], wi_ref[0, :, D_FF + c0:D_FF + c0 + width])
        hidden = ((gate * _sigmoid(gate)) * up).astype(BF16)
        tie = None
        if n < conv_chunks:
            tie = conv_units(n * per_chunk, (n + 1) * per_chunk)
        elif n == conv_chunks:
            c = conv_ref[...]
            mu = jnp.mean(c, axis=-1, keepdims=True)
            d = c - mu
            var = jnp.mean(d * d, axis=-1, keepdims=True)
            y = (d * lax.rsqrt(var + EPS)) * gln_ref[0] + bln_ref[0]
            y = y * _sigmoid(y)
            act_ref[...] = y.astype(BF16)
            tie = [y[r * SUBLANES:(r + 1) * SUBLANES, g * LANES:(g + 1) * LANES]
                   for r in range(row_groups) for g in range(CONV_CH // LANES)]
        if tie is not None:
            o_ref[0:SUBLANES, 0:LANES] += zero_after(tie)
        o_ref[...] += _dot(hidden, wfo_ref[0, c0:c0 + width, :])
        c0 += width


def _post(x, a, sga, sgb, attn, layer, seq, w_dw, b_dw, g_ln, b_ln, w_co, w_ao, w_o, g_ffn, w_fi, w_fo):
    t = x.shape[0]
    tm = ROW_TILE
    n_tiles = t // tm
    per_tile = tm // HALO
    prev = lambda width: pl.BlockSpec((tm, width), lambda s: (jnp.maximum(s - 1, 0), 0))
    cur = lambda s: jnp.minimum(s, n_tiles - 1)
    return pl.pallas_call(
        functools.partial(_post_kernel, tiles_per_seq=seq // tm),
        grid=(n_tiles + 1,),
        in_specs=[prev(D_MODEL),
                  pl.BlockSpec((tm, CONV_CH), lambda s: (cur(s), 0)),
                  pl.BlockSpec((HALO, CONV_CH), lambda s: (jnp.maximum(cur(s) * per_tile - 1, 0), 0)),
                  prev(D_MODEL), prev(D_MODEL), prev(ATTN_WIDTH),
                  _layer_spec((HALO, CONV_CH), layer), _layer_spec((1, CONV_CH), layer),
                  _layer_spec((1, CONV_CH), layer), _layer_spec((1, CONV_CH), layer),
                  _layer_spec((CONV_CH, D_MODEL), layer), _layer_spec((ATTN_WIDTH, D_MODEL), layer),
                  _layer_spec((D_MODEL, D_MODEL), layer), _layer_spec((1, D_MODEL), layer),
                  _layer_spec((D_MODEL, 2 * D_FF), layer), _layer_spec((D_FF, D_MODEL), layer)],
        out_specs=prev(D_MODEL),
        out_shape=jax.ShapeDtypeStruct((t, D_MODEL), F32),
        scratch_shapes=[pltpu.VMEM((tm, CONV_CH), BF16),
                        pltpu.VMEM((CONV_CH // LANES, SUBLANES, HALO + tm, LANES), F32),
                        pltpu.VMEM((tm, CONV_CH), F32),
                        pltpu.VMEM((tm, D_MODEL), BF16)],
        compiler_params=pltpu.CompilerParams(
            dimension_semantics=("arbitrary",), vmem_limit_bytes=VMEM_LIMIT_BYTES),
        name="post",
    )(x, a, a, sga, sgb, attn, w_dw, b_dw, g_ln, b_ln, w_co, w_ao, w_o, g_ffn, w_fi, w_fo)


PACK_CHUNK = 256


def _pack_w_in_kernel(wt_ref, o_ref):
    f0 = 2 * CONV_CH + 3 * ATTN_WIDTH
    gates = 2 * D_MODEL

    def put(col0, row0, n):
        for c in range(0, n, PACK_CHUNK):
            o_ref[0, :, col0 + c:col0 + c + PACK_CHUNK] = (
                wt_ref[0, row0 + c:row0 + c + PACK_CHUNK, :].T.astype(BF16))

    put(0, 0, f0)
    put(f0, f0 + N_HEADS, gates)
    f = wt_ref[0, f0:f0 + N_HEADS, :]
    gap = jnp.zeros((F_STRIDE - N_HEADS, D_MODEL), F32)
    tail = jnp.zeros((F_PAD - F_COPIES * F_STRIDE, D_MODEL), F32)
    o_ref[0, :, f0 + gates:] = jnp.concatenate([f, gap] * F_COPIES + [tail], axis=0).T.astype(BF16)


def _pack_w_in(w_in):
    depth, d, cols = w_in.shape
    return pl.pallas_call(
        _pack_w_in_kernel,
        grid=(depth,),
        in_specs=[pl.BlockSpec((1, cols, d), lambda l: (l, 0, 0), pipeline_mode=pl.Buffered(1))],
        out_specs=pl.BlockSpec((1, d, IN_COLS_PADDED), lambda l: (l, 0, 0)),
        out_shape=jax.ShapeDtypeStruct((depth, d, IN_COLS_PADDED), BF16),
        compiler_params=pltpu.CompilerParams(
            dimension_semantics=("parallel",), vmem_limit_bytes=VMEM_LIMIT_BYTES),
        name="pack_w_in",
    )(jnp.swapaxes(w_in, 1, 2))


def kernel(x, g_mix, w_in, b_forget, w_dw, b_dw, g_conv_ln, b_conv_ln, w_conv_out,
           g_q, g_k, w_attn_out, w_out, g_ffn, w_ffn_in, w_ffn_out):
    bsz, seq, d = x.shape
    depth = w_in.shape[0]

    def spread(f):
        gap = jnp.zeros(f.shape[:-1] + (F_STRIDE - N_HEADS,), f.dtype)
        tail = jnp.zeros(f.shape[:-1] + (F_PAD - F_COPIES * F_STRIDE,), f.dtype)
        return jnp.concatenate([f, gap] * F_COPIES + [tail], axis=-1)

    w_all = _pack_w_in(w_in)
    bf_pad = spread(b_forget)[:, None, :]
    gq = (jnp.tile(g_q, (1, N_HEADS)) * (HEAD_DIM ** -0.5 * LOG2E))[:, None, :]
    gk = jnp.tile(g_k, (1, N_HEADS))[:, None, :]
    w_dw_pad = jnp.pad(w_dw, ((0, 0), (0, HALO - CONV_WIDTH), (0, 0)))
    vec = lambda p: p[:, None, :]
    w_co, w_ao, w_o = w_conv_out.astype(BF16), w_attn_out.astype(BF16), w_out.astype(BF16)
    w_fi, w_fo = w_ffn_in.astype(BF16), w_ffn_out.astype(BF16)

    t = bsz * seq
    for l in range(depth):
        a, qa, ka, v, sga, sgb = _in_proj(x, l, vec(g_mix), w_all, bf_pad, gq, gk)
        attn = _attention(qa, ka, v)
        x = _post(x.reshape(t, d), a.reshape(t, CONV_CH), sga.reshape(t, d), sgb.reshape(t, d),
                  attn.reshape(t, ATTN_WIDTH), l, seq, w_dw_pad, vec(b_dw), vec(g_conv_ln), vec(b_conv_ln),
                  w_co, w_ao, w_o, vec(g_ffn), w_fi, w_fo).reshape(bsz, seq, d)
    return x
```

```python
import functools

import jax
import jax.numpy as jnp
import numpy as np
from jax import lax
from jax.experimental import pallas as pl
from jax.experimental.pallas import tpu as pltpu

D_MODEL = 1024
CONV_CH = 512
CONV_WIDTH = 31
N_HEADS = 8
HEAD_DIM = 64
ATTN_WIDTH = N_HEADS * HEAD_DIM
D_FF = 2816
EPS = 1e-6
LOG2E = 1.4426950408889634

LANES = 128
SUBLANES = 8
HALO = 32
F_PAD = LANES
F_COPIES = 3
F_STRIDE = 16
ONES_LANE = N_HEADS
IN_COLS_PADDED = 2 * CONV_CH + 3 * ATTN_WIDTH + 2 * D_MODEL + F_PAD
VMEM_LIMIT_BYTES = 56 * 1024 * 1024

ROW_TILE = 512
CUMSUM_CHUNK = 256
Q_TILE = 1024
KV_TILE = 512
FFN_CHUNKS = (512, 512, 512, 512, 512, 256)

BF16 = jnp.bfloat16
F32 = jnp.float32


def _dot(a, b):
    return jnp.dot(a, b, preferred_element_type=F32)


def _split3(x):
    hi = x.astype(BF16)
    r = x - hi.astype(F32)
    mid = r.astype(BF16)
    lo = (r - mid.astype(F32)).astype(BF16)
    return hi, mid, lo


def _sigmoid(x):
    return 1.0 / (1.0 + jnp.exp(-x))


def _const_spec(shape):
    nd = len(shape)
    return pl.BlockSpec(shape, lambda *_: (0,) * nd, pipeline_mode=pl.Buffered(1))


def _layer_spec(shape, layer):
    nd = len(shape)
    return pl.BlockSpec((1,) + tuple(shape), lambda *_: (layer,) + (0,) * nd, pipeline_mode=pl.Buffered(1))


def _head_rmsnorm(p, gain):
    rows = p.shape[0]
    low = lax.broadcasted_iota(jnp.int32, (rows, LANES), 1) < HEAD_DIM
    out = []
    for g in range(p.shape[1] // LANES):
        blk = p[:, g * LANES:(g + 1) * LANES]
        sq = blk * blk
        s_low = jnp.sum(jnp.where(low, sq, 0.0), axis=1, keepdims=True)
        s_high = jnp.sum(jnp.where(low, 0.0, sq), axis=1, keepdims=True)
        ms = jnp.where(low, s_low, s_high) * (1.0 / HEAD_DIM)
        out.append(blk * lax.rsqrt(ms + EPS))
    return jnp.concatenate(out, axis=1) * gain


def _in_proj_kernel(x_ref, g_ref, w_ref, bf_ref, gq_ref, gk_ref, tri_ref, place_ref,
                    a_ref, qa_ref, ka_ref, v_ref, sga_ref, sgb_ref, carry_ref):
    tm = x_ref.shape[1]

    @pl.when(pl.program_id(1) == 0)
    def _():
        carry_ref[...] = jnp.zeros_like(carry_ref)

    x = x_ref[0]
    inv = lax.rsqrt(jnp.mean(x * x, axis=-1, keepdims=True) + EPS)
    h = ((x * inv) * g_ref[0]).astype(BF16)

    col_a, col_q = 0, 2 * CONV_CH
    col_k, col_v = col_q + ATTN_WIDTH, col_q + 2 * ATTN_WIDTH
    col_ga = col_v + ATTN_WIDTH
    col_gb, col_f = col_ga + D_MODEL, col_ga + 2 * D_MODEL

    z = _dot(h, w_ref[0, :, col_f:col_f + F_PAD]) + bf_ref[0]
    pq = _dot(h, w_ref[0, :, col_q:col_q + ATTN_WIDTH])
    lane = lax.broadcasted_iota(jnp.int32, (tm, F_PAD), 1)
    valid = (lane < F_COPIES * F_STRIDE) & ((lane & N_HEADS) == 0)
    logf = jnp.where(valid, jnp.minimum(z, 0.0) - jnp.log1p(jnp.exp(-jnp.abs(z))), 0.0)
    hi, mid, lo = _split3(logf)
    parts = jnp.concatenate([hi, mid, lo], axis=1)
    carry = carry_ref[0:1, :]
    chunks = []
    for ch in range(tm // CUMSUM_CHUNK):
        y = _dot(tri_ref[...], parts[ch * CUMSUM_CHUNK:(ch + 1) * CUMSUM_CHUNK])
        c = (y[:, :F_PAD] + y[:, F_PAD:2 * F_PAD]) + y[:, 2 * F_PAD:] + carry
        carry = c[CUMSUM_CHUNK - 1:CUMSUM_CHUNK, :]
        chunks.append(c)
    cum = jnp.concatenate(chunks, axis=0)
    carry_ref[...] = jnp.broadcast_to(carry, carry_ref.shape)
    pk = _dot(h, w_ref[0, :, col_k:col_k + ATTN_WIDTH])
    pa = _dot(h, w_ref[0, :, col_a:col_a + 2 * CONV_CH])
    a_ref[0] = pa[:, :CONV_CH] * _sigmoid(pa[:, CONV_CH:])

    cum1 = jnp.where(lane == ONES_LANE, 1.0, cum * LOG2E)
    hi, mid, lo = _split3(cum1)
    packed = jnp.where(lane < F_STRIDE, hi.astype(F32),
                       jnp.where(lane < 2 * F_STRIDE, mid.astype(F32), lo.astype(F32))).astype(BF16)
    aug = _dot(packed, place_ref[...])
    qn = _head_rmsnorm(pq, gq_ref[0])
    kn = _head_rmsnorm(pk, gk_ref[0])
    low = lax.broadcasted_iota(jnp.int32, (tm, LANES), 1) < HEAD_DIM
    for pr in range(N_HEADS // 2):
        pair = slice(pr * LANES, (pr + 1) * LANES)
        kpair = slice(ATTN_WIDTH + pr * LANES, ATTN_WIDTH + (pr + 1) * LANES)
        qa_ref[0, 2 * pr] = jnp.where(low, qn[:, pair], aug[:, pair]).astype(BF16)
        qa_ref[0, 2 * pr + 1] = jnp.where(low, aug[:, pair], qn[:, pair]).astype(BF16)
        ka_ref[0, 2 * pr] = jnp.where(low, kn[:, pair], aug[:, kpair]).astype(BF16)
        ka_ref[0, 2 * pr + 1] = jnp.where(low, aug[:, kpair], kn[:, pair]).astype(BF16)

    sga_ref[0] = _sigmoid(_dot(h, w_ref[0, :, col_ga:col_ga + D_MODEL])).astype(BF16)
    sgb_ref[0] = _sigmoid(_dot(h, w_ref[0, :, col_gb:col_gb + D_MODEL])).astype(BF16)
    v_ref[0] = _dot(h, w_ref[0, :, col_v:col_v + ATTN_WIDTH]).astype(BF16)


def _placement_matrix():
    p = np.zeros((F_PAD, 2 * ATTN_WIDTH), np.float32)
    for hd in range(N_HEADS):
        base = (hd // 2) * LANES + (HEAD_DIM if hd % 2 == 0 else 0)
        kbase = ATTN_WIDTH + base
        for part in range(F_COPIES):
            p[part * F_STRIDE + hd, base + part] = 1.0
            p[ONES_LANE, base + 3 + part] = 1.0
            p[ONES_LANE, kbase + part] = 1.0
            p[part * F_STRIDE + hd, kbase + 3 + part] = -1.0
    return p


def _in_proj(x, layer, g, w_all, b_forget_pad, gq, gk):
    bsz, seq, _ = x.shape
    tm = ROW_TILE
    tri = jnp.asarray(np.tril(np.ones((CUMSUM_CHUNK, CUMSUM_CHUNK), np.float32)), BF16)
    place = jnp.asarray(_placement_matrix(), BF16)
    row = lambda width: pl.BlockSpec((1, tm, width), lambda b, i: (b, i, 0))
    heads = pl.BlockSpec((1, N_HEADS, tm, LANES), lambda b, i: (b, 0, i, 0))
    return pl.pallas_call(
        _in_proj_kernel,
        grid=(bsz, seq // tm),
        in_specs=[row(D_MODEL), _layer_spec((1, D_MODEL), layer), _layer_spec((D_MODEL, IN_COLS_PADDED), layer),
                  _layer_spec((1, F_PAD), layer), _layer_spec((1, ATTN_WIDTH), layer),
                  _layer_spec((1, ATTN_WIDTH), layer), _const_spec(tri.shape), _const_spec(place.shape)],
        out_specs=[row(CONV_CH), heads, heads, row(ATTN_WIDTH), row(D_MODEL), row(D_MODEL)],
        out_shape=[jax.ShapeDtypeStruct((bsz, seq, CONV_CH), F32),
                   jax.ShapeDtypeStruct((bsz, N_HEADS, seq, LANES), BF16),
                   jax.ShapeDtypeStruct((bsz, N_HEADS, seq, LANES), BF16),
                   jax.ShapeDtypeStruct((bsz, seq, ATTN_WIDTH), BF16),
                   jax.ShapeDtypeStruct((bsz, seq, D_MODEL), BF16),
                   jax.ShapeDtypeStruct((bsz, seq, D_MODEL), BF16)],
        scratch_shapes=[pltpu.VMEM((SUBLANES, F_PAD), F32)],
        compiler_params=pltpu.CompilerParams(
            dimension_semantics=("arbitrary", "arbitrary"), vmem_limit_bytes=VMEM_LIMIT_BYTES),
        name="in_proj",
    )(x, g, w_all, b_forget_pad, gq, gk, tri, place)


def _attn_kernel(qa_ref, ka_ref, v_ref, o_ref, vext_ref, m_ref, acc_ref, sa_ref, sb_ref):
    tq = qa_ref.shape[2]
    tk = KV_TILE
    qi = pl.program_id(2)
    top, bottom, whole = (0, tk), (tk, tq), (0, tq)

    @pl.when(qi == 0)
    def _():
        vext_ref[:, :LANES] = v_ref[0]
        vext_ref[:, LANES:] = jnp.ones((vext_ref.shape[0], LANES), BF16)

    m_ref[...] = jnp.full(m_ref.shape, -jnp.inf, F32)
    acc_ref[...] = jnp.zeros_like(acc_ref)

    def scores(j, s_ref, rows):
        start = pl.multiple_of(j * tk, tk)
        for e in range(2):
            k = ka_ref[0, e, pl.ds(start, tk), :]
            for r0 in range(rows[0], rows[1], tk):
                r = slice(r0, r0 + tk)
                s_ref[e, r] = lax.dot_general(qa_ref[0, e, r], k, (((1,), (1,)), ((), ())),
                                              preferred_element_type=F32)

    def update(j, s_ref, rows, causal):
        start = pl.multiple_of(j * tk, tk)
        vext = vext_ref[pl.ds(start, tk), :]
        for e in range(2):
            for r0 in range(rows[0], rows[1], tk):
                r = slice(r0, r0 + tk)
                s = s_ref[e, r]
                if causal:
                    row = lax.broadcasted_iota(jnp.int32, (tk, tk), 0)
                    col = lax.broadcasted_iota(jnp.int32, (tk, tk), 1)
                    s = jnp.where(col <= row, s, -jnp.inf)
                m_prev = m_ref[e, r]
                m_new = jnp.maximum(m_prev, jnp.max(s, axis=1, keepdims=True))
                alpha = jnp.exp2(m_prev - m_new)
                p = jnp.exp2(s - jnp.concatenate([m_new] * (tk // LANES), axis=1))
                acc_ref[e, r] = jnp.concatenate([alpha, alpha], axis=1) * acc_ref[e, r] + _dot(p.astype(BF16), vext)
                m_ref[e, r] = m_new

    scores(0, sa_ref, whole)

    def pair(t, carry):
        j = 2 * t
        scores(j + 1, sb_ref, whole)
        update(j, sa_ref, whole, causal=False)
        scores(j + 2, sa_ref, whole)
        update(j + 1, sb_ref, whole, causal=False)
        return carry

    lax.fori_loop(0, qi, pair, 0)

    scores(2 * qi + 1, sb_ref, bottom)
    update(2 * qi, sa_ref, top, causal=True)
    start = pl.multiple_of(2 * qi * tk, tk)
    vext2 = vext_ref[pl.ds(start, 2 * tk), :]
    r = slice(*bottom)
    row = lax.broadcasted_iota(jnp.int32, (tk, tk), 0)
    col = lax.broadcasted_iota(jnp.int32, (tk, tk), 1)
    for e in range(2):
        s = jnp.concatenate([sa_ref[e, r], jnp.where(col <= row, sb_ref[e, r], -jnp.inf)], axis=1)
        m_prev = m_ref[e, r]
        m_new = jnp.maximum(m_prev, jnp.max(s, axis=1, keepdims=True))
        alpha = jnp.exp2(m_prev - m_new)
        p = jnp.exp2(s - jnp.concatenate([m_new] * (2 * tk // LANES), axis=1))
        acc_ref[e, r] = jnp.concatenate([alpha, alpha], axis=1) * acc_ref[e, r] + _dot(p.astype(BF16), vext2)

    o0 = acc_ref[0, :, :LANES] / acc_ref[0, :, LANES:]
    o1 = acc_ref[1, :, :LANES] / acc_ref[1, :, LANES:]
    lane = lax.broadcasted_iota(jnp.int32, (tq, LANES), 1)
    o_ref[0] = jnp.where(lane < HEAD_DIM, o0, o1).astype(BF16)


def _attention(qa, ka, v):
    bsz, _, seq, _ = qa.shape
    assert Q_TILE == 2 * KV_TILE
    tq = Q_TILE
    return pl.pallas_call(
        _attn_kernel,
        grid=(bsz, N_HEADS // 2, seq // tq),
        in_specs=[pl.BlockSpec((1, 2, tq, LANES), lambda b, p, i: (b, p, i, 0)),
                  pl.BlockSpec((1, 2, seq, LANES), lambda b, p, i: (b, p, 0, 0)),
                  pl.BlockSpec((1, seq, LANES), lambda b, p, i: (b, 0, p))],
        out_specs=pl.BlockSpec((1, tq, LANES), lambda b, p, i: (b, i, p)),
        out_shape=jax.ShapeDtypeStruct((bsz, seq, ATTN_WIDTH), BF16),
        scratch_shapes=[pltpu.VMEM((seq, 2 * LANES), BF16),
                        pltpu.VMEM((2, tq, LANES), F32),
                        pltpu.VMEM((2, tq, 2 * LANES), F32),
                        pltpu.VMEM((2, tq, KV_TILE), F32),
                        pltpu.VMEM((2, tq, KV_TILE), F32)],
        compiler_params=pltpu.CompilerParams(
            dimension_semantics=("arbitrary", "arbitrary", "arbitrary"), vmem_limit_bytes=VMEM_LIMIT_BYTES),
        name="fox_attention",
    )(qa, ka, v)


def _post_kernel(x_ref, a_ref, halo_ref, sga_ref, sgb_ref, attn_ref,
                 wdw_ref, bdw_ref, gln_ref, bln_ref, wco_ref, wao_ref, wo_ref, gffn_ref, wi_ref, wfo_ref,
                 o_ref, act_ref, win_ref, conv_ref, h_ref, *, tiles_per_seq):
    ts = a_ref.shape[0]
    step = pl.program_id(0)

    @pl.when(step == 0)
    def _():
        act_ref[...] = jnp.zeros_like(act_ref)

    seq_start = (step % tiles_per_seq) == 0
    lead = HALO - (CONV_WIDTH - 1)
    shifted_rows = ts + HALO - SUBLANES
    row_groups = ts // SUBLANES
    n_units = (CONV_CH // LANES) * row_groups

    def zero_after(values):
        bits = [lax.bitcast_convert_type(v, jnp.uint32) for v in values]
        while len(bits) > 1:
            bits = [bits[i] | bits[i + 1] for i in range(0, len(bits) - 1, 2)] + bits[len(bits) & ~1:]
        z = lax.shift_right_logical(lax.shift_right_logical(bits[0], jnp.uint32(16)), jnp.uint32(16))
        return lax.bitcast_convert_type(z, F32)

    def conv_units(first, last):
        accs = []
        for unit in range(first, min(last, n_units)):
            g, r = divmod(unit, row_groups)
            cols = slice(g * LANES, (g + 1) * LANES)
            if r == 0:
                win_ref[g, 0, 0:HALO, :] = jnp.where(seq_start, 0.0, halo_ref[:, cols])
                win_ref[g, 0, HALO:, :] = a_ref[:, cols]
                for k in range(1, SUBLANES):
                    win_ref[g, k, 0:shifted_rows, :] = win_ref[g, 0, k:k + shifted_rows, :]
            acc = jnp.broadcast_to(bdw_ref[0][:, cols], (SUBLANES, LANES))
            for j in range(CONV_WIDTH):
                off = lead + j
                row0 = r * SUBLANES + (off // SUBLANES) * SUBLANES
                acc = acc + wdw_ref[0, j:j + 1, cols] * win_ref[g, off % SUBLANES, row0:row0 + SUBLANES, :]
            conv_ref[r * SUBLANES:(r + 1) * SUBLANES, cols] = acc
            accs.append(acc)
        return accs

    y_conv = _dot(act_ref[...], wco_ref[0])
    y_attn = _dot(attn_ref[...], wao_ref[0])
    merged = sga_ref[...].astype(F32) * y_conv + sgb_ref[...].astype(F32) * y_attn
    x1 = x_ref[...] + _dot(merged.astype(BF16), wo_ref[0])
    inv = lax.rsqrt(jnp.mean(x1 * x1, axis=-1, keepdims=True) + EPS)
    h_ref[...] = ((x1 * inv) * gffn_ref[0]).astype(BF16)
    o_ref[...] = x1
    conv_chunks = len(FFN_CHUNKS) - 2
    per_chunk = -(-n_units // conv_chunks)
    c0 = 0
    for n, width in enumerate(FFN_CHUNKS):
        gate = _dot(h_ref[...], wi_ref[0, :, c0:c0 + width])
        up = _dot(h_ref[...], wi_ref[0, :, D_FF + c0:D_FF + c0 + width])
        hidden = ((gate * _sigmoid(gate)) * up).astype(BF16)
        tie = None
        if n < conv_chunks:
            tie = conv_units(n * per_chunk, (n + 1) * per_chunk)
        elif n == conv_chunks:
            c = conv_ref[...]
            mu = jnp.mean(c, axis=-1, keepdims=True)
            d = c - mu
            var = jnp.mean(d * d, axis=-1, keepdims=True)
            y = (d * lax.rsqrt(var + EPS)) * gln_ref[0] + bln_ref[0]
            y = y * _sigmoid(y)
            act_ref[...] = y.astype(BF16)
            tie = [y[r * SUBLANES:(r + 1) * SUBLANES, g * LANES:(g + 1) * LANES]
                   for r in range(row_groups) for g in range(CONV_CH // LANES)]
        if tie is not None:
            o_ref[0:SUBLANES, 0:LANES] += zero_after(tie)
        o_ref[...] += _dot(hidden, wfo_ref[0, c0:c0 + width, :])
        c0 += width


def _post(x, a, sga, sgb, attn, layer, seq, w_dw, b_dw, g_ln, b_ln, w_co, w_ao, w_o, g_ffn, w_fi, w_fo):
    t = x.shape[0]
    tm = ROW_TILE
    n_tiles = t // tm
    per_tile = tm // HALO
    prev = lambda width: pl.BlockSpec((tm, width), lambda s: (jnp.maximum(s - 1, 0), 0))
    cur = lambda s: jnp.minimum(s, n_tiles - 1)
    return pl.pallas_call(
        functools.partial(_post_kernel, tiles_per_seq=seq // tm),
        grid=(n_tiles + 1,),
        in_specs=[prev(D_MODEL),
                  pl.BlockSpec((tm, CONV_CH), lambda s: (cur(s), 0)),
                  pl.BlockSpec((HALO, CONV_CH), lambda s: (jnp.maximum(cur(s) * per_tile - 1, 0), 0)),
                  prev(D_MODEL), prev(D_MODEL), prev(ATTN_WIDTH),
                  _layer_spec((HALO, CONV_CH), layer), _layer_spec((1, CONV_CH), layer),
                  _layer_spec((1, CONV_CH), layer), _layer_spec((1, CONV_CH), layer),
                  _layer_spec((CONV_CH, D_MODEL), layer), _layer_spec((ATTN_WIDTH, D_MODEL), layer),
                  _layer_spec((D_MODEL, D_MODEL), layer), _layer_spec((1, D_MODEL), layer),
                  _layer_spec((D_MODEL, 2 * D_FF), layer), _layer_spec((D_FF, D_MODEL), layer)],
        out_specs=prev(D_MODEL),
        out_shape=jax.ShapeDtypeStruct((t, D_MODEL), F32),
        scratch_shapes=[pltpu.VMEM((tm, CONV_CH), BF16),
                        pltpu.VMEM((CONV_CH // LANES, SUBLANES, HALO + tm, LANES), F32),
                        pltpu.VMEM((tm, CONV_CH), F32),
                        pltpu.VMEM((tm, D_MODEL), BF16)],
        compiler_params=pltpu.CompilerParams(
            dimension_semantics=("arbitrary",), vmem_limit_bytes=VMEM_LIMIT_BYTES),
        name="post",
    )(x, a, a, sga, sgb, attn, w_dw, b_dw, g_ln, b_ln, w_co, w_ao, w_o, g_ffn, w_fi, w_fo)


PACK_CHUNK = 256


def _pack_w_in_kernel(wt_ref, o_ref):
    f0 = 2 * CONV_CH + 3 * ATTN_WIDTH
    gates = 2 * D_MODEL

    def put(col0, row0, n):
        for c in range(0, n, PACK_CHUNK):
            o_ref[0, :, col0 + c:col0 + c + PACK_CHUNK] = (
                wt_ref[0, row0 + c:row0 + c + PACK_CHUNK, :].T.astype(BF16))

    put(0, 0, f0)
    put(f0, f0 + N_HEADS, gates)
    f = wt_ref[0, f0:f0 + N_HEADS, :]
    gap = jnp.zeros((F_STRIDE - N_HEADS, D_MODEL), F32)
    tail = jnp.zeros((F_PAD - F_COPIES * F_STRIDE, D_MODEL), F32)
    o_ref[0, :, f0 + gates:] = jnp.concatenate([f, gap] * F_COPIES + [tail], axis=0).T.astype(BF16)


def _pack_w_in(w_in):
    depth, d, cols = w_in.shape
    return pl.pallas_call(
        _pack_w_in_kernel,
        grid=(depth,),
        in_specs=[pl.BlockSpec((1, cols, d), lambda l: (l, 0, 0), pipeline_mode=pl.Buffered(1))],
        out_specs=pl.BlockSpec((1, d, IN_COLS_PADDED), lambda l: (l, 0, 0)),
        out_shape=jax.ShapeDtypeStruct((depth, d, IN_COLS_PADDED), BF16),
        compiler_params=pltpu.CompilerParams(
            dimension_semantics=("parallel",), vmem_limit_bytes=VMEM_LIMIT_BYTES),
        name="pack_w_in",
    )(jnp.swapaxes(w_in, 1, 2))


def kernel(x, g_mix, w_in, b_forget, w_dw, b_dw, g_conv_ln, b_conv_ln, w_conv_out,
           g_q, g_k, w_attn_out, w_out, g_ffn, w_ffn_in, w_ffn_out):
    bsz, seq, d = x.shape
    depth = w_in.shape[0]

    def spread(f):
        gap = jnp.zeros(f.shape[:-1] + (F_STRIDE - N_HEADS,), f.dtype)
        tail = jnp.zeros(f.shape[:-1] + (F_PAD - F_COPIES * F_STRIDE,), f.dtype)
        return jnp.concatenate([f, gap] * F_COPIES + [tail], axis=-1)

    w_all = _pack_w_in(w_in)
    bf_pad = spread(b_forget)[:, None, :]
    gq = (jnp.tile(g_q, (1, N_HEADS)) * (HEAD_DIM ** -0.5 * LOG2E))[:, None, :]
    gk = jnp.tile(g_k, (1, N_HEADS))[:, None, :]
    w_dw_pad = jnp.pad(w_dw, ((0, 0), (0, HALO - CONV_WIDTH), (0, 0)))
    vec = lambda p: p[:, None, :]
    w_co, w_ao, w_o = w_conv_out.astype(BF16), w_attn_out.astype(BF16), w_out.astype(BF16)
    w_fi, w_fo = w_ffn_in.astype(BF16), w_ffn_out.astype(BF16)

    t = bsz * seq
    for l in range(depth):
        a, qa, ka, v, sga, sgb = _in_proj(x, l, vec(g_mix), w_all, bf_pad, gq, gk)
        attn = _attention(qa, ka, v)
        x = _post(x.reshape(t, d), a.reshape(t, CONV_CH), sga.reshape(t, d), sgb.reshape(t, d),
                  attn.reshape(t, ATTN_WIDTH), l, seq, w_dw_pad, vec(b_dw), vec(g_conv_ln), vec(b_conv_ln),
                  w_co, w_ao, w_o, vec(g_ffn), w_fi, w_fo).reshape(bsz, seq, d)
    return x
```
